```python
import math
import functools
import jax
import jax.numpy as jnp
from jax import lax
import numpy as np

D_MODEL = 1024
BATCH = 16
SEQ = 2048
DEPTH = 2

GRID_W = 64
CTX_LEN = 256
F32 = jnp.float32

HY_W = 384
HY_ORDER = 2
HY_SHORT = 3
HY_EMB = 33
HY_HID = 64
RET_HEADS = 6
RET_DH = 64
RET_W = RET_HEADS * RET_DH
RET_CHUNK = 64
ROPE_BASE = 10000.0
ML_HEADS = 4
ML_DH = 64
ML_W = ML_HEADS * ML_DH
ML_SHORT = 3
ML_CHUNK = 64

MIX_W = HY_W + RET_W + ML_W
N_BRANCH = 3
HY_OFF = 0
RET_OFF = HY_OFF + 3 * HY_W
ML_OFF = RET_OFF + 4 * RET_W
MLG_OFF = ML_OFF + 4 * ML_W
GATE_OFF = MLG_OFF + 4 * ML_HEADS
N_IN = GATE_OFF + N_BRANCH * D_MODEL

D_FF = ((8 * D_MODEL // 3 + 127) // 128) * 128
N_EXPERTS = 8
TOP_K = 2
N_DENSE = (DEPTH + 1) // 2
N_MOE = DEPTH // 2
EPS = 1e-6

kernel_name = 'hybrid_hyena_retnet_mlstm_moe_dit'


def rmsnorm(x, g):
    xf = x.astype(F32)
    y = xf * lax.rsqrt(jnp.mean(xf * xf, axis=-1, keepdims=True) + EPS)
    return (y * g.astype(F32)).astype(x.dtype)


def modulate(h, shift, scale):
    return h * (1 + scale) + shift


def head_norm(x):
    mu = jnp.mean(x, axis=-1, keepdims=True)
    xc = x - mu
    return xc * lax.rsqrt(jnp.mean(xc * xc, axis=-1, keepdims=True) + EPS)


def split_heads(x, n_heads):
    b, l, w = x.shape
    return x.reshape(b, l, n_heads, w // n_heads).transpose(0, 2, 1, 3)


def merge_heads(x):
    b, h, l, d = x.shape
    return x.transpose(0, 2, 1, 3).reshape(b, l, h * d)


def short_conv(u, w):
    k_w = w.shape[0]
    l = u.shape[1]
    pad = k_w // 2
    up = jnp.pad(u, ((0, 0), (pad, k_w - 1 - pad), (0, 0)))
    return sum(up[:, j:j + l] * w[j] for j in range(k_w))


def to_chunks(x, ch):
    b, h, l = x.shape[:3]
    x = x.reshape(b, h, l // ch, ch, *x.shape[3:])
    return jnp.moveaxis(x, 2, 0)


def from_chunks(y):
    n, b, h, ch = y.shape[:4]
    return jnp.moveaxis(y, 0, 2).reshape(b, h, n * ch, *y.shape[4:])


def flip_seq(a):
    return jnp.flip(a, axis=2)


def hyena_filters(length, p):
    pos = jnp.arange(length, dtype=F32)
    t = pos / max(length - 1, 1)
    n_bands = (HY_EMB - 1) // 2
    bands = jnp.linspace(1e-4, n_bands - 1, n_bands, dtype=F32)
    z = (2.0 * math.pi * pos / length)[:, None] * bands[None, :]
    feats = jnp.concatenate([t[:, None], jnp.cos(z), -jnp.sin(z)], axis=-1)
    hid = jnp.sin(feats @ p['hy_f1'].astype(F32) + p['hy_fb1'].astype(F32))
    hid = jnp.sin(hid @ p['hy_f2'].astype(F32) + p['hy_fb2'].astype(F32))
    h = hid @ p['hy_f3'].astype(F32)
    window = jnp.exp(-t[:, None] * jnp.abs(p['hy_decay'].astype(F32))[None, :])
    return (h * window).reshape(length, HY_ORDER, 2, HY_W)


def long_conv(u, h_fwd, h_bwd, skip):
    b, l, ch = u.shape
    filt = jnp.concatenate([h_fwd, jnp.zeros((1, ch), F32), h_bwd[1:][::-1]], axis=0)
    u_f = jnp.fft.rfft(u.astype(F32), n=2 * l, axis=1)
    k_f = jnp.fft.rfft(filt, axis=0)
    y = jnp.fft.irfft(u_f * k_f[None], n=2 * l, axis=1)[:, :l]
    return y + u.astype(F32) * skip.astype(F32)


def hyena_branch(u, p):
    length = u.shape[1]
    u = short_conv(u, p['hy_conv'])
    v, x1, x2 = jnp.split(u, 3, axis=-1)
    filt = hyena_filters(length, p)
    z = v.astype(F32)
    for o, gate in enumerate((x1, x2)):
        z = gate.astype(F32) * long_conv(z, filt[:, o, 0], filt[:, o, 1], p['hy_skip'][o])
    return z


def rotary_tables(row_id, col_id):
    q4 = RET_DH // 4
    inv = 1.0 / (ROPE_BASE ** (jnp.arange(q4, dtype=F32) / q4))
    ang_r = row_id.astype(F32)[:, None] * inv[None, :]
    ang_c = col_id.astype(F32)[:, None] * inv[None, :]
    return (jnp.cos(ang_r), jnp.sin(ang_r), jnp.cos(ang_c), jnp.sin(ang_c))


def apply_axial_rotary(x, rope):
    cos_r, sin_r, cos_c, sin_c = rope
    half = RET_DH // 2
    q4 = half // 2

    def rot(y, cs, sn):
        y1, y2 = y[..., :q4], y[..., q4:]
        return jnp.concatenate([y1 * cs - y2 * sn, y1 * sn + y2 * cs], axis=-1)

    return jnp.concatenate([rot(x[..., :half], cos_r, sin_r), rot(x[..., half:], cos_c, sin_c)], axis=-1)


def retention_inputs(u, rope):
    seg = u[..., RET_OFF:ML_OFF].astype(F32)
    q, k, v, g = jnp.split(seg, 4, axis=-1)
    q = split_heads(q, RET_HEADS) * RET_DH ** -0.5
    k = split_heads(k, RET_HEADS)
    v = split_heads(v, RET_HEADS)
    if rope is not None:
        q = apply_axial_rotary(q, rope)
        k = apply_axial_rotary(k, rope)
    return q, k, v, g


def retention_scan(q, k, v, log_g, s0):
    idx = jnp.arange(RET_CHUNK, dtype=F32)
    rel = idx[:, None] - idx[None, :]
    causal = rel >= 0
    intra_decay = jnp.where(causal, jnp.exp(jnp.where(causal, rel, 0.0) * log_g[:, None, None]), 0.0)
    q_decay = jnp.exp((idx + 1.0) * log_g[:, None])
    k_decay = jnp.exp((RET_CHUNK - 1.0 - idx) * log_g[:, None])
    chunk_decay = jnp.exp(RET_CHUNK * log_g)[:, None, None]

    def step(s, qkv):
        qc, kc, vc = qkv
        scores = jnp.einsum('bhtd,bhsd->bhts', qc, kc) * intra_decay
        o = jnp.einsum('bhts,bhse->bhte', scores, vc) + jnp.einsum('bhtd,bhde->bhte', qc, s) * q_decay[..., None]
        s = s * chunk_decay + jnp.einsum('bhsd,bhse->bhde', kc * k_decay[..., None], vc)
        return s, o

    s_fin, o = lax.scan(step, s0, (to_chunks(q, RET_CHUNK), to_chunks(k, RET_CHUNK), to_chunks(v, RET_CHUNK)))
    return from_chunks(o), s_fin


def retention_bidir(q, k, v, log_gamma, s_fwd0, s_bwd0):
    o_f, s_f = retention_scan(q, k, v, log_gamma[0], s_fwd0)
    o_b, s_b = retention_scan(flip_seq(q), flip_seq(k), flip_seq(v), log_gamma[1], s_bwd0)
    return o_f + flip_seq(o_b), s_f, s_b


def retention_out(o, g):
    return merge_heads(head_norm(o)) * jax.nn.silu(g)


def mlstm_inputs(u, p):
    b, l = u.shape[:2]
    qk = jax.nn.silu(short_conv(u[..., ML_OFF:ML_OFF + 2 * ML_W], p['ml_conv'])).astype(F32)
    q = split_heads(qk[..., :ML_W], ML_HEADS)
    k = split_heads(qk[..., ML_W:], ML_HEADS) * ML_DH ** -0.5
    v = split_heads(u[..., ML_OFF + 2 * ML_W:ML_OFF + 3 * ML_W].astype(F32), ML_HEADS)
    o = u[..., ML_OFF + 3 * ML_W:MLG_OFF].astype(F32)
    gates = u[..., MLG_OFF:GATE_OFF].astype(F32).reshape(b, l, 2, 2, ML_HEADS) + p['ml_gate_bias'].astype(F32)
    gates = jnp.moveaxis(gates, 1, -1)
    return q, k, v, o, gates[:, :, 0], jax.nn.log_sigmoid(gates[:, :, 1])


def mlstm_scan(q, k, v, ig, lf, state):
    idx = jnp.arange(ML_CHUNK)
    causal = idx[:, None] >= idx[None, :]

    def step(carry, inp):
        c_mat, n_vec, m = carry
        qc, kc, vc, ic, fc = inp
        b = jnp.cumsum(fc, axis=-1)
        dlog = jnp.where(causal, b[..., :, None] - b[..., None, :] + ic[..., None, :], -jnp.inf)
        inter = b + m[..., None]
        m_t = jnp.maximum(inter, jnp.max(dlog, axis=-1))
        w = jnp.exp(dlog - m_t[..., None])
        a = jnp.exp(inter - m_t)
        s = jnp.einsum('bhtd,bhsd->bhts', qc, kc) * w
        num = jnp.einsum('bhts,bhse->bhte', s, vc) + a[..., None] * jnp.einsum('bhtd,bhde->bhte', qc, c_mat)
        den = jnp.sum(s, axis=-1) + a * jnp.einsum('bhtd,bhd->bht', qc, n_vec)
        h = num / jnp.maximum(jnp.abs(den), jnp.exp(-m_t))[..., None]
        b_last = b[..., -1]
        wlog = b_last[..., None] - b + ic
        m_new = jnp.maximum(b_last + m, jnp.max(wlog, axis=-1))
        wk = jnp.exp(wlog - m_new[..., None])
        a_s = jnp.exp(b_last + m - m_new)
        c_mat = a_s[..., None, None] * c_mat + jnp.einsum('bhs,bhsd,bhse->bhde', wk, kc, vc)
        n_vec = a_s[..., None] * n_vec + jnp.einsum('bhs,bhsd->bhd', wk, kc)
        return (c_mat, n_vec, m_new), h

    xs = (to_chunks(q, ML_CHUNK), to_chunks(k, ML_CHUNK), to_chunks(v, ML_CHUNK),
          to_chunks(ig, ML_CHUNK), to_chunks(lf, ML_CHUNK))
    state, h = lax.scan(step, state, xs)
    return from_chunks(h), state


def mlstm_bidir(q, k, v, ig, lf, st_fwd0, st_bwd0):
    h_f, st_f = mlstm_scan(q, k, v, ig[:, 0], lf[:, 0], st_fwd0)
    h_b, st_b = mlstm_scan(flip_seq(q), flip_seq(k), flip_seq(v), flip_seq(ig[:, 1]), flip_seq(lf[:, 1]), st_bwd0)
    return h_f + flip_seq(h_b), st_f, st_b


def mlstm_out(h, o):
    return jax.nn.sigmoid(o) * merge_heads(head_norm(h))


def merge_branches(u, ys, p):
    b, l = u.shape[:2]
    gates = jax.nn.sigmoid(u[..., GATE_OFF:].reshape(b, l, N_BRANCH, D_MODEL))
    bounds = (0, HY_W, HY_W + RET_W, MIX_W)
    acc = 0
    for i, y in enumerate(ys):
        acc = acc + gates[:, :, i] * (y.astype(u.dtype) @ p['w_up'][bounds[i]:bounds[i + 1]])
    return acc @ p['w_out']


def mixer(h_lat, h_ctx, rope, p, need_ctx):
    batch = h_lat.shape[0]
    u_lat = h_lat @ p['w_in'] + p['b_in']
    u_ctx = h_ctx @ p['w_in'] + p['b_in']
    log_gamma = jax.nn.log_sigmoid(p['ret_decay'].astype(F32))
    s0 = jnp.zeros((batch, RET_HEADS, RET_DH, RET_DH), F32)
    q_c, k_c, v_c, g_c = retention_inputs(u_ctx, None)
    r_ctx, s_fwd, s_bwd = retention_bidir(q_c, k_c, v_c, log_gamma, s0, s0)
    q_l, k_l, v_l, g_l = retention_inputs(u_lat, rope)
    r_lat, _, _ = retention_bidir(q_l, k_l, v_l, log_gamma, s_fwd, s_bwd)
    st0 = (jnp.zeros((batch, ML_HEADS, ML_DH, ML_DH), F32),
           jnp.zeros((batch, ML_HEADS, ML_DH), F32),
           jnp.zeros((batch, ML_HEADS), F32))
    mq_c, mk_c, mv_c, mo_c, ig_c, lf_c = mlstm_inputs(u_ctx, p)
    m_ctx, st_fwd, st_bwd = mlstm_bidir(mq_c, mk_c, mv_c, ig_c, lf_c, st0, st0)
    mq_l, mk_l, mv_l, mo_l, ig_l, lf_l = mlstm_inputs(u_lat, p)
    m_lat, _, _ = mlstm_bidir(mq_l, mk_l, mv_l, ig_l, lf_l, st_fwd, st_bwd)
    y_lat = merge_branches(u_lat, (hyena_branch(u_lat[..., HY_OFF:RET_OFF], p),
                                   retention_out(r_lat, g_l),
                                   mlstm_out(m_lat, mo_l)), p)
    if not need_ctx:
        return y_lat, None
    y_ctx = merge_branches(u_ctx, (hyena_branch(u_ctx[..., HY_OFF:RET_OFF], p),
                                   retention_out(r_ctx, g_c),
                                   mlstm_out(m_ctx, mo_c)), p)
    return y_lat, y_ctx


def swiglu(h, w1, w3, w2):
    return (jax.nn.silu(h @ w1) * (h @ w3)) @ w2


def moe_ffn(h, w_router, b_router, w1, w3, w2):
    logits = (h @ w_router + b_router).astype(F32)
    top_val, top_idx = lax.top_k(logits, TOP_K)
    top_w = jax.nn.softmax(top_val, axis=-1)
    combine = jnp.sum(jax.nn.one_hot(top_idx, N_EXPERTS, dtype=F32) * top_w[..., None], axis=-2).astype(h.dtype)
    out = 0
    for e in range(N_EXPERTS):
        out = out + combine[..., e:e + 1] * swiglu(h, w1[e], w3[e], w2[e])
    return out


def setup_inputs(seed: int = 0) -> dict:
    key = jax.random.key(seed)
    ks = jax.random.split(key, 40)

    def nrm(k, shape, s):
        return jax.random.normal(k, shape, F32) * s

    x = nrm(ks[0], (BATCH, SEQ, D_MODEL), 1.0)
    c = nrm(ks[1], (BATCH, D_MODEL), 1.0)
    ctx = nrm(ks[2], (BATCH, CTX_LEN, D_MODEL), 1.0)
    c_ctx = nrm(ks[3], (D_MODEL,), 1.0)
    w_mod = nrm(ks[4], (DEPTH, D_MODEL, 6 * D_MODEL), 0.5 * D_MODEL ** -0.5)
    b_mod = nrm(ks[5], (DEPTH, 6 * D_MODEL), 0.02)
    g_mix = 1.0 + nrm(ks[6], (DEPTH, D_MODEL), 0.02)
    g_ffn = 1.0 + nrm(ks[7], (DEPTH, D_MODEL), 0.02)
    w_in = nrm(ks[8], (DEPTH, D_MODEL, N_IN), D_MODEL ** -0.5)
    b_in = nrm(ks[9], (DEPTH, N_IN), 0.02)
    hy_conv = nrm(ks[10], (DEPTH, HY_SHORT, 3 * HY_W), HY_SHORT ** -0.5)
    hy_f1 = nrm(ks[11], (DEPTH, HY_EMB, HY_HID), HY_EMB ** -0.5)
    hy_fb1 = nrm(ks[12], (DEPTH, HY_HID), 0.1)
    hy_f2 = nrm(ks[13], (DEPTH, HY_HID, HY_HID), HY_HID ** -0.5)
    hy_fb2 = nrm(ks[14], (DEPTH, HY_HID), 0.1)
    hy_f3 = nrm(ks[15], (DEPTH, HY_HID, HY_ORDER * 2 * HY_W), 0.01)
    base_decay = jnp.linspace(abs(math.log(1e-2)) / 1.5, abs(math.log(1e-2)) / 0.3, HY_W, dtype=F32)
    base_decay = jnp.broadcast_to(base_decay[None, :], (HY_ORDER * 2, HY_W)).reshape(-1)
    hy_decay = base_decay * (1.0 + nrm(ks[16], (DEPTH, HY_ORDER * 2 * HY_W), 0.05))
    hy_skip = nrm(ks[17], (DEPTH, HY_ORDER, HY_W), 0.5)
    gamma = 1.0 - 2.0 ** (-5.0 - jnp.arange(RET_HEADS, dtype=F32))
    ret_decay = jnp.log(gamma / (1.0 - gamma)) + nrm(ks[18], (DEPTH, 2, RET_HEADS), 0.1)
    ml_conv = nrm(ks[19], (DEPTH, ML_SHORT, 2 * ML_W), ML_SHORT ** -0.5)
    ig_bias = nrm(ks[20], (DEPTH, 2, ML_HEADS), 0.1)
    fg_bias = jnp.linspace(3.0, 6.0, ML_HEADS, dtype=F32) + nrm(ks[21], (DEPTH, 2, ML_HEADS), 0.1)
    ml_gate_bias = jnp.stack([ig_bias, fg_bias], axis=2)
    row_scale = jnp.concatenate([jnp.full((HY_W,), HY_W ** -0.5, F32),
                                 jnp.full((RET_W,), RET_W ** -0.5, F32),
                                 jnp.full((ML_W,), ML_W ** -0.5, F32)])
    w_up = nrm(ks[22], (DEPTH, MIX_W, D_MODEL), 1.0) * row_scale[:, None]
    w_out = nrm(ks[23], (DEPTH, D_MODEL, D_MODEL), D_MODEL ** -0.5)
    ffn_w1 = nrm(ks[24], (N_DENSE, D_MODEL, D_FF), D_MODEL ** -0.5)
    ffn_w3 = nrm(ks[25], (N_DENSE, D_MODEL, D_FF), D_MODEL ** -0.5)
    ffn_w2 = nrm(ks[26], (N_DENSE, D_FF, D_MODEL), D_FF ** -0.5)
    moe_router = nrm(ks[27], (N_MOE, D_MODEL, N_EXPERTS), D_MODEL ** -0.5)
    moe_router_b = nrm(ks[28], (N_MOE, N_EXPERTS), 0.01)
    moe_w1 = nrm(ks[29], (N_MOE, N_EXPERTS, D_MODEL, D_FF), D_MODEL ** -0.5)
    moe_w3 = nrm(ks[30], (N_MOE, N_EXPERTS, D_MODEL, D_FF), D_MODEL ** -0.5)
    moe_w2 = nrm(ks[31], (N_MOE, N_EXPERTS, D_FF, D_MODEL), D_FF ** -0.5)
    g_final = 1.0 + nrm(ks[32], (D_MODEL,), 0.02)
    return {'x': x, 'c': c, 'ctx': ctx, 'c_ctx': c_ctx, 'w_mod': w_mod, 'b_mod': b_mod,
            'g_mix': g_mix, 'g_ffn': g_ffn, 'w_in': w_in, 'b_in': b_in, 'hy_conv': hy_conv,
            'hy_f1': hy_f1, 'hy_fb1': hy_fb1, 'hy_f2': hy_f2, 'hy_fb2': hy_fb2, 'hy_f3': hy_f3,
            'hy_decay': hy_decay, 'hy_skip': hy_skip, 'ret_decay': ret_decay, 'ml_conv': ml_conv,
            'ml_gate_bias': ml_gate_bias, 'w_up': w_up, 'w_out': w_out, 'ffn_w1': ffn_w1,
            'ffn_w3': ffn_w3, 'ffn_w2': ffn_w2, 'moe_router': moe_router, 'moe_router_b': moe_router_b,
            'moe_w1': moe_w1, 'moe_w3': moe_w3, 'moe_w2': moe_w2, 'g_final': g_final}


def reference(x, c, ctx, c_ctx, w_mod, b_mod, g_mix, g_ffn, w_in, b_in, hy_conv, hy_f1, hy_fb1,
              hy_f2, hy_fb2, hy_f3, hy_decay, hy_skip, ret_decay, ml_conv, ml_gate_bias, w_up, w_out,
              ffn_w1, ffn_w3, ffn_w2, moe_router, moe_router_b, moe_w1, moe_w3, moe_w2, g_final):
    batch, n_lat, _ = x.shape
    ROWS = n_lat // GRID_W
    row_id = jnp.broadcast_to(jnp.arange(ROWS)[:, None], (ROWS, GRID_W)).reshape(-1)
    col_id = jnp.broadcast_to(jnp.arange(GRID_W)[None, :], (ROWS, GRID_W)).reshape(-1)
    rope = rotary_tables(row_id, col_id)
    s_lat = jax.nn.silu(c)
    s_ctx = jax.nn.silu(c_ctx)
    xl, xc = x, ctx
    for layer in range(DEPTH):
        last = layer == DEPTH - 1
        mod_l = (s_lat @ w_mod[layer] + b_mod[layer]).reshape(batch, 6, 1, D_MODEL)
        mod_c = (s_ctx @ w_mod[layer] + b_mod[layer]).reshape(6, D_MODEL)
        p = {'w_in': w_in[layer], 'b_in': b_in[layer], 'hy_conv': hy_conv[layer],
             'hy_f1': hy_f1[layer], 'hy_fb1': hy_fb1[layer], 'hy_f2': hy_f2[layer],
             'hy_fb2': hy_fb2[layer], 'hy_f3': hy_f3[layer], 'hy_decay': hy_decay[layer],
             'hy_skip': hy_skip[layer], 'ret_decay': ret_decay[layer], 'ml_conv': ml_conv[layer],
             'ml_gate_bias': ml_gate_bias[layer], 'w_up': w_up[layer], 'w_out': w_out[layer]}
        hl = modulate(rmsnorm(xl, g_mix[layer]), mod_l[:, 0], mod_l[:, 1])
        hc = modulate(rmsnorm(xc, g_mix[layer]), mod_c[0], mod_c[1])
        yl, yc = mixer(hl, hc, rope, p, not last)
        xl = xl + mod_l[:, 2] * yl
        if not last:
            xc = xc + mod_c[2] * yc
        j = layer // 2
        if layer % 2 == 0:
            ffn = functools.partial(swiglu, w1=ffn_w1[j], w3=ffn_w3[j], w2=ffn_w2[j])
        else:
            ffn = functools.partial(moe_ffn, w_router=moe_router[j], b_router=moe_router_b[j],
                                    w1=moe_w1[j], w3=moe_w3[j], w2=moe_w2[j])
        fl = modulate(rmsnorm(xl, g_ffn[layer]), mod_l[:, 3], mod_l[:, 4])
        xl = xl + mod_l[:, 5] * ffn(fl)
        if not last:
            fc = modulate(rmsnorm(xc, g_ffn[layer]), mod_c[3], mod_c[4])
            xc = xc + mod_c[5] * ffn(fc)
    return rmsnorm(xl, g_final)
```

```python
import functools
import math

import jax
import jax.numpy as jnp
from jax import lax
from jax.experimental import pallas as pl
from jax.experimental.pallas import tpu as pltpu

F32 = jnp.float32
BF16 = jnp.bfloat16
EPS = 1e-6

GRID_W = 64
HY_W = 384
RET_HEADS, RET_DH = 6, 64
RET_W = RET_HEADS * RET_DH
ML_HEADS, ML_DH = 4, 64
ML_W = ML_HEADS * ML_DH
HY_EMB = 33
ROPE_BASE = 10000.0
N_EXPERTS = 8
LANES = 128

GATE_OFF = 0
RETM_OFF = 3 * 1024
MLM_OFF = RETM_OFF + 4 * RET_W
MLG_M_OFF = MLM_OFF + 4 * ML_W
N_MAIN = MLG_M_OFF + LANES
N_TILE = 1152
CHUNK = 256

VMEM_LIMIT = 56 * 1024 * 1024


def _cp(sem, vmem=None):
    return pltpu.CompilerParams(dimension_semantics=sem, vmem_limit_bytes=vmem)


def _split3(a):
    hi = a.astype(BF16)
    r1 = a - hi.astype(F32)
    mid = r1.astype(BF16)
    lo = (r1 - mid.astype(F32)).astype(BF16)
    return hi, mid, lo


def _dot(a, b):
    return jnp.dot(a, b, preferred_element_type=F32)


def _dot_nt(a, b):
    return lax.dot_general(a, b, (((1,), (1,)), ((), ())), preferred_element_type=F32)


def _dot3(a, b):
    ah, am, al = _split3(a)
    bh, bm, bl = _split3(b)
    return (_dot(ah, bh) + (_dot(ah, bm) + _dot(am, bh))
            + (_dot(am, bm) + _dot(ah, bl) + _dot(al, bh)))


def _sigmoid(x):
    return 1.0 / (1.0 + jnp.exp(-x))


def _silu(x):
    return x * _sigmoid(x)


def _log_sigmoid(x):
    return jnp.minimum(x, 0.0) - jnp.log(1.0 + jnp.exp(-jnp.abs(x)))


def _mod_kernel(c_ref, w_ref, b_ref, o_ref):
    s = _silu(c_ref[...])
    o_ref[0] = _dot3(s, w_ref[0]) + b_ref[0]


def _mod(c_all, w_mod, b_mod):
    depth, d, n = w_mod.shape
    mp = c_all.shape[0]
    tn = 1536
    return pl.pallas_call(
        _mod_kernel,
        grid=(depth, n // tn),
        in_specs=[pl.BlockSpec((mp, d), lambda l, j: (0, 0)),
                  pl.BlockSpec((1, d, tn), lambda l, j: (l, 0, j)),
                  pl.BlockSpec((1, 1, tn), lambda l, j: (l, 0, j))],
        out_specs=pl.BlockSpec((1, mp, tn), lambda l, j: (l, 0, j)),
        out_shape=jax.ShapeDtypeStruct((depth, mp, n), F32),
        compiler_params=_cp(("parallel", "parallel"), VMEM_LIMIT),
        name="mod",
    )(c_all, w_mod, b_mod.reshape(depth, 1, n))


def _norm_mod(x, g, shift, scale):
    y = x * lax.rsqrt(jnp.mean(x * x, axis=-1, keepdims=True) + EPS)
    return (y * g) * (1.0 + scale) + shift


def _inproj_kernel(x_ref, g_ref, sh_ref, sc_ref, wm_ref, bm_ref, wh_ref, bh_ref,
                   um_ref, uh_ref, h_scr):
    n = pl.program_id(2)

    @pl.when(n == 0)
    def _():
        h = _norm_mod(x_ref[0], g_ref[...], sh_ref[0], sc_ref[0]).astype(BF16)
        h_scr[...] = h
        uh_ref[0] = _dot_nt(wh_ref[...], h) + bh_ref[...]

    um_ref[0] = _dot(h_scr[...], wm_ref[...]) + bm_ref[...]


def _in_proj(x, g, shift, scale, w_main, b_main, w_hyt, b_hy, tm):
    bx, lx, d = x.shape
    per_batch_mod = shift.shape[0] > 1
    mi = (lambda b, i, n: (b, 0, 0)) if per_batch_mod else (lambda b, i, n: (0, 0, 0))
    nh = w_hyt.shape[0]
    return pl.pallas_call(
        _inproj_kernel,
        grid=(bx, lx // tm, N_MAIN // N_TILE),
        in_specs=[pl.BlockSpec((1, tm, d), lambda b, i, n: (b, i, 0)),
                  pl.BlockSpec((1, d), lambda b, i, n: (0, 0)),
                  pl.BlockSpec((1, 1, d), mi),
                  pl.BlockSpec((1, 1, d), mi),
                  pl.BlockSpec((d, N_TILE), lambda b, i, n: (0, n)),
                  pl.BlockSpec((1, N_TILE), lambda b, i, n: (0, n)),
                  pl.BlockSpec((nh, d), lambda b, i, n: (0, 0)),
                  pl.BlockSpec((nh, 1), lambda b, i, n: (0, 0))],
        out_specs=[pl.BlockSpec((1, tm, N_TILE), lambda b, i, n: (b, i, n)),
                   pl.BlockSpec((1, nh, tm), lambda b, i, n: (b, 0, i))],
        out_shape=[jax.ShapeDtypeStruct((bx, lx, N_MAIN), F32),
                   jax.ShapeDtypeStruct((bx, nh, lx), F32)],
        scratch_shapes=[pltpu.VMEM((tm, d), BF16)],
        compiler_params=_cp(("parallel", "parallel", "arbitrary"), VMEM_LIMIT),
        name="in_proj",
    )(x, g, shift, scale, w_main, b_main, w_hyt, b_hy)


def _dft_kernel(h_ref, f_ref, o_ref):
    hh, hm, hl = _split3(h_ref[...])
    f = f_ref[0]
    o_ref[0] = _dot(hh, f) + _dot(hm, f) + _dot(hl, f)


def _dft(h_rows, fwd):
    r, l = h_rows.shape
    nf, _, tw = fwd.shape
    return pl.pallas_call(
        _dft_kernel,
        grid=(nf,),
        in_specs=[pl.BlockSpec((r, l), lambda f: (0, 0)),
                  pl.BlockSpec((1, l, tw), lambda f: (f, 0, 0))],
        out_specs=pl.BlockSpec((1, r, tw), lambda f: (f, 0, 0)),
        out_shape=jax.ShapeDtypeStruct((nf, r, tw), F32),
        compiler_params=_cp(("parallel",), VMEM_LIMIT),
        name="dft",
    )(h_rows, fwd)


def _short_conv_lanes(u, w):
    l = u.shape[1]
    t = lax.broadcasted_iota(jnp.int32, u.shape, 1)
    prev = jnp.where(t == 0, 0.0, pltpu.roll(u, 1, 1))
    nxt = jnp.where(t == l - 1, 0.0, pltpu.roll(u, l - 1, 1))
    return prev * w[:, 0:1] + u * w[:, 1:2] + nxt * w[:, 2:3]


def _hyena_kernel(in_ref, gate_ref, wi_ref, wg_ref, skip_ref, f_ref, coef_ref, o_ref,
                  inf_scr, inb_scr, y_scr, acc_scr, *, conv_in, bt, cw, tf):
    f = pl.program_id(1)
    nf = pl.num_programs(1)

    @pl.when(f == 0)
    def _():
        for i in range(bt):
            u = in_ref[i]
            if conv_in:
                u = _short_conv_lanes(u, wi_ref[...])
            inf_scr[i * cw:(i + 1) * cw, :] = u
            inb_scr[i * cw:(i + 1) * cw, :] = u.astype(BF16)
        acc_scr[...] = jnp.zeros_like(acc_scr)

    fw = f_ref[0]
    spec = _dot(inb_scr[...], fw)
    ca, cb, cc, cd = coef_ref[0, 0], coef_ref[0, 1], coef_ref[0, 2], coef_ref[0, 3]
    for i in range(bt):
        ur = spec[i * cw:(i + 1) * cw, :tf]
        ui = spec[i * cw:(i + 1) * cw, tf:]
        y_scr[i * cw:(i + 1) * cw, :tf] = (ur * ca + ui * cb).astype(BF16)
        y_scr[i * cw:(i + 1) * cw, tf:] = (ur * cc + ui * cd).astype(BF16)
    acc_scr[...] += _dot_nt(y_scr[...], fw)

    @pl.when(f == nf - 1)
    def _():
        for i in range(bt):
            gate = _short_conv_lanes(gate_ref[i], wg_ref[...])
            z = acc_scr[i * cw:(i + 1) * cw, :] + skip_ref[...] * inf_scr[i * cw:(i + 1) * cw, :]
            o_ref[i] = gate * z


def _hyena_conv(src, src_blk, gate_src, gate_blk, w_conv_t, skip, fwd, coef, conv_in, bt):
    b, _, l = src.shape
    cw = HY_W
    nf, _, tw = fwd.shape
    tf = tw // 2
    kern = functools.partial(_hyena_kernel, conv_in=conv_in, bt=bt, cw=cw, tf=tf)
    return pl.pallas_call(
        kern,
        grid=(b // bt, nf),
        in_specs=[pl.BlockSpec((bt, cw, l), lambda i, f: (i, src_blk, 0)),
                  pl.BlockSpec((bt, cw, l), lambda i, f: (i, gate_blk, 0)),
                  pl.BlockSpec((cw, 3), lambda i, f: (src_blk if conv_in else 0, 0)),
                  pl.BlockSpec((cw, 3), lambda i, f: (gate_blk, 0)),
                  pl.BlockSpec((cw, 1), lambda i, f: (0, 0)),
                  pl.BlockSpec((1, l, tw), lambda i, f: (f, 0, 0)),
                  pl.BlockSpec((1, 4, cw, tf), lambda i, f: (f, 0, 0, 0))],
        out_specs=pl.BlockSpec((bt, cw, l), lambda i, f: (i, 0, 0)),
        out_shape=jax.ShapeDtypeStruct((b, cw, l), F32),
        scratch_shapes=[pltpu.VMEM((bt * cw, l), F32),
                        pltpu.VMEM((bt * cw, l), BF16),
                        pltpu.VMEM((bt * cw, tw), BF16),
                        pltpu.VMEM((bt * cw, l), F32)],
        compiler_params=_cp(("parallel", "arbitrary"), VMEM_LIMIT),
        name="hyena_conv",
    )(src, gate_src, w_conv_t, w_conv_t, skip, fwd, coef)


def _dft_matrix(l, tf):
    n = jnp.arange(l, dtype=jnp.int32)[:, None]
    k = jnp.arange(l, dtype=jnp.int32)[None, :]
    ang = ((n * k) % (2 * l)).astype(F32) * (math.pi / l)
    re = jnp.cos(ang)
    im = -jnp.sin(ang)
    nyq = jnp.where(n % 2 == 0, 1.0, -1.0).astype(F32)
    im = jnp.where(k == 0, nyq, im)
    nf = l // tf
    re = re.reshape(l, nf, tf)
    im = im.reshape(l, nf, tf)
    return jnp.concatenate([re, im], axis=-1).transpose(1, 0, 2).astype(BF16)


def _hyena_filters(length, f1, fb1, f2, fb2, f3, decay):
    hp = lax.Precision.HIGHEST
    pos = jnp.arange(length, dtype=F32)
    t = pos / max(length - 1, 1)
    n_bands = (HY_EMB - 1) // 2
    bands = jnp.linspace(1e-4, n_bands - 1, n_bands, dtype=F32)
    z = (2.0 * math.pi * pos / length)[:, None] * bands[None, :]
    feats = jnp.concatenate([t[:, None], jnp.cos(z), -jnp.sin(z)], axis=-1)
    hid = jnp.sin(jnp.dot(feats, f1, precision=hp) + fb1)
    hid = jnp.sin(jnp.dot(hid, f2, precision=hp) + fb2)
    h = jnp.dot(hid, f3, precision=hp)
    window = jnp.exp(-t[:, None] * jnp.abs(decay)[None, :])
    return h * window


def _filter_coefs(filt, fwd):
    l = filt.shape[0]
    nf, _, tw = fwd.shape
    tf = tw // 2
    rows = filt.T
    is_bwd = (jnp.arange(rows.shape[0]) // HY_W) % 2 == 1
    first = jnp.arange(l)[None, :] == 0
    rows = jnp.where(is_bwd[:, None] & first, 0.0, rows)
    spec = _dft(rows, fwd)
    spec = spec.reshape(nf, 2, 2, HY_W, 2, tf)
    sf, sb = spec[:, :, 0], spec[:, :, 1]
    kre = sf[..., 0, :] + sb[..., 0, :]
    kim = sf[..., 1, :] - sb[..., 1, :]
    knyq = sf[..., 1, :] + sb[..., 1, :]
    kbin = (jnp.arange(nf)[:, None] * tf + jnp.arange(tf)[None, :])
    is0 = (kbin == 0)[:, None, None, :]
    wgt = jnp.where(is0, 1.0, 2.0) / (2.0 * l)
    ca = kre * wgt
    cb = jnp.where(is0, 0.0, -kim) * wgt
    cc = jnp.where(is0, 0.0, kim) * wgt
    cd = jnp.where(is0, knyq, kre) * wgt
    coef = jnp.stack([ca, cb, cc, cd], axis=2)
    return coef.transpose(1, 0, 2, 3, 4)


def _mlqk_kernel(u_ref, w_ref, o_ref):
    u = u_ref[0]
    l = u.shape[0]
    t = lax.broadcasted_iota(jnp.int32, u.shape, 0)
    prev = jnp.where(t == 0, 0.0, pltpu.roll(u, 1, 0))
    nxt = jnp.where(t == l - 1, 0.0, pltpu.roll(u, l - 1, 0))
    y = _silu(prev * w_ref[0:1, :] + u * w_ref[1:2, :] + nxt * w_ref[2:3, :])
    lane = lax.broadcasted_iota(jnp.int32, u.shape, 1)
    o_ref[0] = jnp.where(lane >= ML_W, y * (ML_DH ** -0.5), y).astype(BF16)


def _mlqk(u_main, ml_conv):
    b, l, _ = u_main.shape
    w = 2 * ML_W
    return pl.pallas_call(
        _mlqk_kernel,
        grid=(b,),
        in_specs=[pl.BlockSpec((1, l, w), lambda i: (i, 0, MLM_OFF // w)),
                  pl.BlockSpec((3, w), lambda i: (0, 0))],
        out_specs=pl.BlockSpec((1, l, w), lambda i: (i, 0, 0)),
        out_shape=jax.ShapeDtypeStruct((b, l, w), BF16),
        compiler_params=_cp(("parallel",), VMEM_LIMIT),
        name="mlqk",
    )(u_main, ml_conv)


def _head_norm_pair(xb, lo):
    def halves(v):
        s_all = jnp.sum(v, axis=1, keepdims=True)
        s_lo = jnp.sum(jnp.where(lo, v, 0.0), axis=1, keepdims=True)
        return jnp.where(lo, s_lo, s_all - s_lo) * (1.0 / 64.0)
    xc = xb - halves(xb)
    return xc * lax.rsqrt(halves(xc * xc) + EPS)


def _scan_kernel(*refs, reverse, use_rope, c):
    it = iter(refs)
    lg_ref = next(it)
    ret_ref = next(it)
    if use_rope:
        cos_ref, sa_ref, sb_ref = next(it), next(it), next(it)
    mlqk_ref = next(it)
    mlvo_ref = next(it)
    mlg_ref = next(it)
    gb_ref = next(it)
    rs0_ref, mc0_ref, mn0_ref, mm0_ref = next(it), next(it), next(it), next(it)
    if reverse:
        of_ref, hf_ref = next(it), next(it)
    o_ref, h_ref = next(it), next(it)
    rs_out, mc_out, mn_out, mm_out = next(it), next(it), next(it), next(it)
    dec_scr, qd_scr, kd_scr, tri_scr, rs_scr, mc_scr, mn_scr, mm_scr = (
        next(it), next(it), next(it), next(it), next(it), next(it), next(it), next(it))

    ci = pl.program_id(1)
    nc = pl.num_programs(1)
    d = 1 if reverse else 0

    tt = lax.broadcasted_iota(jnp.int32, (c, c), 0)
    ss = lax.broadcasted_iota(jnp.int32, (c, c), 1)
    rel = (ss - tt) if reverse else (tt - ss)
    valid = rel >= 0
    lane = lax.broadcasted_iota(jnp.int32, (c, LANES), 1)
    lo = lane < 64
    trow = lax.broadcasted_iota(jnp.int32, (c, LANES), 0)
    r128 = lax.broadcasted_iota(jnp.int32, (LANES, LANES), 0)
    c128 = lax.broadcasted_iota(jnp.int32, (LANES, LANES), 1)
    bdiag = (r128 < 64) == (c128 < 64)
    lo_row = lax.broadcasted_iota(jnp.int32, (1, LANES), 1) < 64

    @pl.when(ci == 0)
    def _():
        relf = jnp.where(valid, rel, 0).astype(F32)
        qpow = ((c - trow) if reverse else (trow + 1)).astype(F32)
        kpow = (trow if reverse else (c - 1 - trow)).astype(F32)
        for h in range(RET_HEADS):
            dec_scr[h] = jnp.where(valid, jnp.exp(relf * lg_ref[h]), 0.0)
        for p in range(RET_HEADS // 2):
            lg_l = jnp.where(lo, lg_ref[2 * p], lg_ref[2 * p + 1])
            qd_scr[p] = jnp.exp(qpow * lg_l)
            kd_scr[p] = jnp.exp(kpow * lg_l)
        tri_scr[...] = jnp.where(valid, 1.0, 0.0).astype(BF16)
        rs_scr[...] = rs0_ref[0]
        mc_scr[...] = mc0_ref[0]
        mn_scr[...] = mn0_ref[0]
        mm_scr[...] = mm0_ref[0]

    u = ret_ref[0]
    for p in range(RET_HEADS // 2):
        sl = slice(p * LANES, (p + 1) * LANES)
        q = u[:, p * LANES:(p + 1) * LANES] * (RET_DH ** -0.5)
        k = u[:, RET_W + p * LANES:RET_W + (p + 1) * LANES]
        v = u[:, 2 * RET_W + p * LANES:2 * RET_W + (p + 1) * LANES].astype(BF16)
        if use_rope:
            cs, sa, sb = cos_ref[:, sl], sa_ref[:, sl], sb_ref[:, sl]
            q = q * cs + pltpu.roll(q, LANES - 16, 1) * sa + pltpu.roll(q, 16, 1) * sb
            k = k * cs + pltpu.roll(k, LANES - 16, 1) * sa + pltpu.roll(k, 16, 1) * sb
        qb = q.astype(BF16)
        kb = k.astype(BF16)
        outs = []
        for hh in range(2):
            qm = jnp.where(lo == (hh == 0), qb, jnp.zeros_like(qb))
            s = _dot_nt(qm, kb) * dec_scr[2 * p + hh]
            outs.append(_dot(s.astype(BF16), v))
        o_intra = jnp.where(lo, outs[0], outs[1])
        st = rs_scr[p]
        o_inter = _dot(qb, st.astype(BF16)) * qd_scr[p]
        o_new = o_intra + o_inter
        kw_t = (k * kd_scr[p]).T.astype(BF16)
        upd = _dot(kw_t, v)
        cdec = jnp.exp(jnp.where(lo_row, lg_ref[2 * p], lg_ref[2 * p + 1]) * float(c))
        rs_scr[p] = st * cdec + jnp.where(bdiag, upd, 0.0)
        if reverse:
            o_tot = of_ref[0, :, sl] + o_new
            g = u[:, 3 * RET_W + p * LANES:3 * RET_W + (p + 1) * LANES]
            o_ref[0, :, sl] = _head_norm_pair(o_tot, lo) * _silu(g)
        else:
            o_ref[0, :, sl] = o_new

    gates = mlg_ref[0] + gb_ref[...]
    lf = _log_sigmoid(gates)
    tri = tri_scr[...]
    l_hi, l_mid, l_lo = _split3(lf)
    cum = _dot(tri, l_hi) + _dot(tri, l_mid) + _dot(tri, l_lo)
    cum_t = cum.T
    gates_t = gates.T
    last = 0 if reverse else c - 1
    qk = mlqk_ref[0]
    vo = mlvo_ref[0]
    for p in range(ML_HEADS // 2):
        sl = slice(p * LANES, (p + 1) * LANES)
        qb = qk[:, p * LANES:(p + 1) * LANES]
        kb = qk[:, ML_W + p * LANES:ML_W + (p + 1) * LANES]
        v = vo[:, p * LANES:(p + 1) * LANES].astype(BF16)
        cst = mc_scr[p]
        nrow = mn_scr[p]
        inter_num = _dot(qb, cst.astype(BF16))
        qf = qb.astype(F32)
        kf = kb.astype(F32)
        qn = qf * nrow
        nums, dens, a_l, m_l, wk_l, as_l = [], [], [], [], [], []
        for hh in range(2):
            h = 2 * p + hh
            ic = d * 8 + h
            fc = d * 8 + 4 + h
            b_col = cum[:, fc:fc + 1]
            b_row = cum_t[fc:fc + 1, :]
            i_row = gates_t[ic:ic + 1, :]
            i_col = gates[:, ic:ic + 1]
            m_prev = mm_scr[h:h + 1, 0:1]
            dlog = jnp.where(valid, b_col - b_row + i_row, -jnp.inf)
            inter = b_col + m_prev
            m_t = jnp.maximum(inter, jnp.max(dlog, axis=1, keepdims=True))
            w = jnp.exp(dlog - m_t)
            a = jnp.exp(inter - m_t)
            qm = jnp.where(lo == (hh == 0), qb, jnp.zeros_like(qb))
            s = _dot_nt(qm, kb) * w
            nums.append(_dot(s.astype(BF16), v))
            qn_h = jnp.sum(jnp.where(lo == (hh == 0), qn, 0.0), axis=1, keepdims=True)
            dens.append(jnp.sum(s, axis=1, keepdims=True) + a * qn_h)
            a_l.append(a)
            m_l.append(m_t)
            total = cum[last:last + 1, fc:fc + 1]
            wlog = total - b_col + i_col
            m_new = jnp.maximum(total + m_prev, jnp.max(wlog, axis=0, keepdims=True))
            wk_l.append(jnp.exp(wlog - m_new))
            as_l.append(jnp.exp(total + m_prev - m_new))
            mm_scr[h:h + 1, :] = jnp.broadcast_to(m_new, (1, LANES))
        a_lanes = jnp.where(lo, a_l[0], a_l[1])
        num = jnp.where(lo, nums[0], nums[1]) + a_lanes * inter_num
        den = jnp.where(lo, dens[0], dens[1])
        m_lanes = jnp.where(lo, m_l[0], m_l[1])
        h_new = num / jnp.maximum(jnp.abs(den), jnp.exp(-m_lanes))
        kw = kf * jnp.where(lo, wk_l[0], wk_l[1])
        as_row = jnp.where(lo_row, as_l[0], as_l[1])
        upd = _dot(kw.T.astype(BF16), v)
        mc_scr[p] = cst * as_row + jnp.where(bdiag, upd, 0.0)
        mn_scr[p] = nrow * as_row + jnp.sum(kw, axis=0, keepdims=True)
        if reverse:
            h_tot = hf_ref[0, :, sl] + h_new
            og = vo[:, ML_W + p * LANES:ML_W + (p + 1) * LANES]
            h_ref[0, :, sl] = _sigmoid(og) * _head_norm_pair(h_tot, lo)
        else:
            h_ref[0, :, sl] = h_new

    @pl.when(ci == nc - 1)
    def _():
        rs_out[0] = rs_scr[...]
        mc_out[0] = mc_scr[...]
        mn_out[0] = mn_scr[...]
        mm_out[0] = mm_scr[...]


def _scan(u_main, mlqk, rope, lg, gate_bias, states, prev, reverse):
    b, l, _ = u_main.shape
    c = min(CHUNK, l)
    nc = l // c
    use_rope = rope is not None
    cix = (lambda i: nc - 1 - i) if reverse else (lambda i: i)
    in_specs = [pl.BlockSpec(memory_space=pltpu.SMEM),
                pl.BlockSpec((1, c, 4 * RET_W), lambda bi, i: (bi, cix(i), RETM_OFF // (4 * RET_W)))]
    args = [lg, u_main]
    if use_rope:
        in_specs += [pl.BlockSpec((c, RET_W), lambda bi, i: (cix(i), 0))] * 3
        args += list(rope)
    in_specs += [pl.BlockSpec((1, c, 2 * ML_W), lambda bi, i: (bi, cix(i), 0)),
                 pl.BlockSpec((1, c, 2 * ML_W), lambda bi, i: (bi, cix(i), MLM_OFF // (2 * ML_W) + 1)),
                 pl.BlockSpec((1, c, LANES), lambda bi, i: (bi, cix(i), MLG_M_OFF // LANES)),
                 pl.BlockSpec((1, LANES), lambda bi, i: (0, 0)),
                 pl.BlockSpec((1, 3, LANES, LANES), lambda bi, i: (bi, 0, 0, 0)),
                 pl.BlockSpec((1, 2, LANES, LANES), lambda bi, i: (bi, 0, 0, 0)),
                 pl.BlockSpec((1, 2, 1, LANES), lambda bi, i: (bi, 0, 0, 0)),
                 pl.BlockSpec((1, 8, LANES), lambda bi, i: (bi, 0, 0))]
    args += [mlqk, u_main, u_main, gate_bias, *states]
    if reverse:
        in_specs += [pl.BlockSpec((1, c, RET_W), lambda bi, i: (bi, cix(i), 0)),
                     pl.BlockSpec((1, c, ML_W), lambda bi, i: (bi, cix(i), 0))]
        args += list(prev)
    out_specs = [pl.BlockSpec((1, c, RET_W), lambda bi, i: (bi, cix(i), 0)),
                 pl.BlockSpec((1, c, ML_W), lambda bi, i: (bi, cix(i), 0)),
                 pl.BlockSpec((1, 3, LANES, LANES), lambda bi, i: (bi, 0, 0, 0)),
                 pl.BlockSpec((1, 2, LANES, LANES), lambda bi, i: (bi, 0, 0, 0)),
                 pl.BlockSpec((1, 2, 1, LANES), lambda bi, i: (bi, 0, 0, 0)),
                 pl.BlockSpec((1, 8, LANES), lambda bi, i: (bi, 0, 0))]
    out_shape = [jax.ShapeDtypeStruct((b, l, RET_W), F32),
                 jax.ShapeDtypeStruct((b, l, ML_W), F32),
                 jax.ShapeDtypeStruct((b, 3, LANES, LANES), F32),
                 jax.ShapeDtypeStruct((b, 2, LANES, LANES), F32),
                 jax.ShapeDtypeStruct((b, 2, 1, LANES), F32),
                 jax.ShapeDtypeStruct((b, 8, LANES), F32)]
    scratch = [pltpu.VMEM((RET_HEADS, c, c), F32),
               pltpu.VMEM((3, c, LANES), F32),
               pltpu.VMEM((3, c, LANES), F32),
               pltpu.VMEM((c, c), BF16),
               pltpu.VMEM((3, LANES, LANES), F32),
               pltpu.VMEM((2, LANES, LANES), F32),
               pltpu.VMEM((2, 1, LANES), F32),
               pltpu.VMEM((8, LANES), F32)]
    kern = functools.partial(_scan_kernel, reverse=reverse, use_rope=use_rope, c=c)
    outs = pl.pallas_call(
        kern, grid=(b, nc), in_specs=in_specs, out_specs=out_specs, out_shape=out_shape,
        scratch_shapes=scratch,
        compiler_params=_cp(("parallel", "arbitrary"), VMEM_LIMIT),
        name="scan_bwd" if reverse else "scan_fwd",
    )(*args)
    return (outs[0], outs[1]), tuple(outs[2:])


def _merge_kernel(yh_ref, yr_ref, ym_ref, gt_ref, x_ref, mg_ref, wa_ref, wb_ref, wc_ref, wo_ref, o_ref):
    d = x_ref.shape[2]
    g = gt_ref[0]
    yh = yh_ref[0].T.astype(BF16)
    acc = _sigmoid(g[:, 0:d]) * _dot(yh, wa_ref[...])
    acc += _sigmoid(g[:, d:2 * d]) * _dot(yr_ref[0].astype(BF16), wb_ref[...])
    acc += _sigmoid(g[:, 2 * d:3 * d]) * _dot(ym_ref[0].astype(BF16), wc_ref[...])
    y = _dot(acc.astype(BF16), wo_ref[...])
    o_ref[0] = x_ref[0] + mg_ref[0] * y


def _merge(y_hyt, y_ret, y_ml, u_main, x, mgate, w_up, w_out, tm):
    b, l, d = x.shape
    per_batch_mod = mgate.shape[0] > 1
    mi = (lambda bi, i: (bi, 0, 0)) if per_batch_mod else (lambda bi, i: (0, 0, 0))
    wa, wb, wc = w_up[:HY_W], w_up[HY_W:HY_W + RET_W], w_up[HY_W + RET_W:]
    const = lambda bi, i: (0, 0)
    return pl.pallas_call(
        _merge_kernel,
        grid=(b, l // tm),
        in_specs=[pl.BlockSpec((1, HY_W, tm), lambda bi, i: (bi, 0, i)),
                  pl.BlockSpec((1, tm, RET_W), lambda bi, i: (bi, i, 0)),
                  pl.BlockSpec((1, tm, ML_W), lambda bi, i: (bi, i, 0)),
                  pl.BlockSpec((1, tm, 3 * d), lambda bi, i: (bi, i, 0)),
                  pl.BlockSpec((1, tm, d), lambda bi, i: (bi, i, 0)),
                  pl.BlockSpec((1, 1, d), mi),
                  pl.BlockSpec(wa.shape, const),
                  pl.BlockSpec(wb.shape, const),
                  pl.BlockSpec(wc.shape, const),
                  pl.BlockSpec(w_out.shape, const)],
        out_specs=pl.BlockSpec((1, tm, d), lambda bi, i: (bi, i, 0)),
        out_shape=jax.ShapeDtypeStruct((b, l, d), F32),
        compiler_params=_cp(("parallel", "parallel"), VMEM_LIMIT),
        name="merge",
    )(y_hyt, y_ret, y_ml, u_main, x, mgate, wa, wb, wc, w_out)


def _route(h, wr, br):
    logits = _dot3(h, wr) + br
    lane = lax.broadcasted_iota(jnp.int32, logits.shape, 1)
    neg = jnp.float32(-jnp.inf)
    logits = jnp.where(lane < N_EXPERTS, logits, neg)
    m1 = jnp.max(logits, axis=1, keepdims=True)
    i1 = jnp.min(jnp.where(logits == m1, lane, LANES), axis=1, keepdims=True)
    rest = jnp.where(lane == i1, neg, logits)
    m2 = jnp.max(rest, axis=1, keepdims=True)
    i2 = jnp.min(jnp.where(rest == m2, lane, LANES), axis=1, keepdims=True)
    e2 = jnp.exp(m2 - m1)
    w1 = 1.0 / (1.0 + e2)
    w2 = e2 / (1.0 + e2)
    return jnp.where(lane == i1, w1, 0.0) + jnp.where(lane == i2, w2, 0.0)


def _ffn_kernel(*refs, moe, final):
    it = iter(refs)
    x_ref, g_ref, sh_ref, sc_ref, mg_ref = next(it), next(it), next(it), next(it), next(it)
    if moe:
        wr_ref, br_ref = next(it), next(it)
    w1_ref, w3_ref, w2_ref = next(it), next(it), next(it)
    if final:
        gf_ref = next(it)
    o_ref = next(it)
    h_scr, acc_scr = next(it), next(it)
    if moe:
        cmb_scr = next(it)
    e = pl.program_id(2)
    f = pl.program_id(3)
    first = jnp.logical_and(e == 0, f == 0)
    lastp = jnp.logical_and(e == pl.num_programs(2) - 1, f == pl.num_programs(3) - 1)

    @pl.when(first)
    def _():
        h = _norm_mod(x_ref[0], g_ref[...], sh_ref[0], sc_ref[0])
        h_scr[...] = h.astype(BF16)
        acc_scr[...] = jnp.zeros_like(acc_scr)
        if moe:
            cmb_scr[...] = _route(h, wr_ref[...], br_ref[...])

    h = h_scr[...]
    act = _silu(_dot(h, w1_ref[0])) * _dot(h, w3_ref[0])
    if moe:
        lane = lax.broadcasted_iota(jnp.int32, cmb_scr.shape, 1)
        ce = jnp.sum(jnp.where(lane == e, cmb_scr[...], 0.0), axis=1, keepdims=True)
        act = act * ce
    acc_scr[...] += _dot(act.astype(BF16), w2_ref[0])

    @pl.when(lastp)
    def _():
        y = x_ref[0] + mg_ref[0] * acc_scr[...]
        if final:
            y = y * lax.rsqrt(jnp.mean(y * y, axis=-1, keepdims=True) + EPS) * gf_ref[...]
        o_ref[0] = y


def _ffn(x, g, shift, scale, mgate, w1, w3, w2, router=None, g_final=None, tm=512, tf=1408):
    b, l, d = x.shape
    ne, _, ff = w1.shape
    moe = router is not None
    final = g_final is not None
    per_batch_mod = shift.shape[0] > 1
    mi = (lambda bi, i, e, f: (bi, 0, 0)) if per_batch_mod else (lambda bi, i, e, f: (0, 0, 0))
    const2 = lambda bi, i, e, f: (0, 0)
    in_specs = [pl.BlockSpec((1, tm, d), lambda bi, i, e, f: (bi, i, 0)),
                pl.BlockSpec((1, d), const2),
                pl.BlockSpec((1, 1, d), mi), pl.BlockSpec((1, 1, d), mi), pl.BlockSpec((1, 1, d), mi)]
    args = [x, g, shift, scale, mgate]
    if moe:
        in_specs += [pl.BlockSpec((d, LANES), const2), pl.BlockSpec((1, LANES), const2)]
        args += list(router)
    in_specs += [pl.BlockSpec((1, d, tf), lambda bi, i, e, f: (e, 0, f)),
                 pl.BlockSpec((1, d, tf), lambda bi, i, e, f: (e, 0, f)),
                 pl.BlockSpec((1, tf, d), lambda bi, i, e, f: (e, f, 0))]
    args += [w1, w3, w2]
    if final:
        in_specs += [pl.BlockSpec((1, d), const2)]
        args += [g_final]
    scratch = [pltpu.VMEM((tm, d), BF16), pltpu.VMEM((tm, d), F32)]
    if moe:
        scratch += [pltpu.VMEM((tm, LANES), F32)]
    return pl.pallas_call(
        functools.partial(_ffn_kernel, moe=moe, final=final),
        grid=(b, l // tm, ne, ff // tf),
        in_specs=in_specs,
        out_specs=pl.BlockSpec((1, tm, d), lambda bi, i, e, f: (bi, i, 0)),
        out_shape=jax.ShapeDtypeStruct((b, l, d), F32),
        scratch_shapes=scratch,
        compiler_params=_cp(("parallel", "parallel", "arbitrary", "arbitrary"), VMEM_LIMIT),
        name="ffn_moe" if moe else "ffn",
    )(*args)


def _rope_tables(n_lat):
    rows = n_lat // GRID_W
    row_id = jnp.broadcast_to(jnp.arange(rows)[:, None], (rows, GRID_W)).reshape(-1).astype(F32)
    col_id = jnp.broadcast_to(jnp.arange(GRID_W)[None, :], (rows, GRID_W)).reshape(-1).astype(F32)
    q4 = RET_DH // 4
    inv = 1.0 / (ROPE_BASE ** (jnp.arange(q4, dtype=F32) / q4))
    ang_r = row_id[:, None] * inv[None, :]
    ang_c = col_id[:, None] * inv[None, :]
    zeros = jnp.zeros_like(ang_r)
    cos_h = jnp.concatenate([jnp.cos(ang_r)] * 2 + [jnp.cos(ang_c)] * 2, axis=-1)
    sin_a = jnp.concatenate([-jnp.sin(ang_r), zeros, -jnp.sin(ang_c), zeros], axis=-1)
    sin_b = jnp.concatenate([zeros, jnp.sin(ang_r), zeros, jnp.sin(ang_c)], axis=-1)
    tile = lambda t: jnp.tile(t, (1, RET_HEADS))
    return tile(cos_h), tile(sin_a), tile(sin_b)


def _zero_states(b):
    return (jnp.zeros((b, 3, LANES, LANES), F32), jnp.zeros((b, 2, LANES, LANES), F32),
            jnp.zeros((b, 2, 1, LANES), F32), jnp.zeros((b, 8, LANES), F32))


def kernel(x, c, ctx, c_ctx, w_mod, b_mod, g_mix, g_ffn, w_in, b_in, hy_conv, hy_f1, hy_fb1, hy_f2,
           hy_fb2, hy_f3, hy_decay, hy_skip, ret_decay, ml_conv, ml_gate_bias, w_up, w_out, ffn_w1,
           ffn_w3, ffn_w2, moe_router, moe_router_b, moe_w1, moe_w3, moe_w2, g_final):
    b, l, d = x.shape
    lc = ctx.shape[1]
    depth = w_mod.shape[0]
    hy_off, ret_off = 0, 3 * HY_W
    ml_off = ret_off + 4 * RET_W
    mlg_off = ml_off + 4 * ML_W
    gate_off = mlg_off + 4 * ML_HEADS

    mp = ((b + 1 + 7) // 8) * 8
    c_all = jnp.zeros((mp, d), F32).at[:b].set(c).at[b].set(c_ctx)
    mods = _mod(c_all, w_mod, b_mod).reshape(depth, mp, 6, d)

    rope = _rope_tables(l)
    tf_l, tf_c = min(256, l), min(256, lc)
    fwd_l = _dft_matrix(l, tf_l)
    fwd_c = _dft_matrix(lc, tf_c)
    tm_l = min(1024, l)
    tm_c = min(1024, b * lc)

    xl, xc = x, ctx
    for layer in range(depth):
        last = layer == depth - 1
        ml_ = mods[layer, :b][:, :, None, :]
        mc_ = mods[layer, b][None, :, None, :]
        wi = w_in[layer]
        bi = b_in[layer]
        w_main = jnp.concatenate(
            [wi[:, gate_off:], wi[:, ret_off:ml_off], wi[:, ml_off:mlg_off], wi[:, mlg_off:gate_off],
             jnp.zeros((d, LANES - 4 * ML_HEADS), F32)], axis=1).astype(BF16)
        b_main = jnp.concatenate(
            [bi[gate_off:], bi[ret_off:ml_off], bi[ml_off:mlg_off], bi[mlg_off:gate_off],
             jnp.zeros((LANES - 4 * ML_HEADS,), F32)])[None, :]
        w_hyt = wi[:, hy_off:ret_off].T.astype(BF16)
        b_hy = bi[hy_off:ret_off][:, None]
        gmix = g_mix[layer][None, :]

        um_l, uh_l = _in_proj(xl, gmix, ml_[:, 0], ml_[:, 1], w_main, b_main, w_hyt, b_hy, tm_l)
        xc_flat = xc.reshape(1, b * lc, d)
        um_c, uh_c = _in_proj(xc_flat, gmix, mc_[:, 0], mc_[:, 1], w_main, b_main, w_hyt, b_hy, tm_c)
        um_c = um_c.reshape(b, lc, N_MAIN)

        lg = jax.nn.log_sigmoid(ret_decay[layer].astype(F32))
        gbias = jnp.zeros((1, LANES), F32).at[0, :4 * ML_HEADS].set(ml_gate_bias[layer].reshape(-1))
        qk_c = _mlqk(um_c, ml_conv[layer])
        qk_l = _mlqk(um_l, ml_conv[layer])
        z0 = _zero_states(b)
        pf_c, sf = _scan(um_c, qk_c, None, lg[0], gbias, z0, None, False)
        yc_rm, sb = _scan(um_c, qk_c, None, lg[1], gbias, z0, pf_c, True)
        pf_l, _ = _scan(um_l, qk_l, rope, lg[0], gbias, sf, None, False)
        yl_rm, _ = _scan(um_l, qk_l, rope, lg[1], gbias, sb, pf_l, True)

        w_conv_t = hy_conv[layer].T
        skip = hy_skip[layer][:, :, None]

        def hyena(uh, length, fwd, bt):
            filt = _hyena_filters(length, hy_f1[layer], hy_fb1[layer], hy_f2[layer], hy_fb2[layer],
                                  hy_f3[layer], hy_decay[layer])
            coef = _filter_coefs(filt, fwd)
            z1 = _hyena_conv(uh, 0, uh, 1, w_conv_t, skip[0], fwd, coef[0], True, bt)
            return _hyena_conv(z1, 0, uh, 2, w_conv_t, skip[1], fwd, coef[1], False, bt)

        yh_l = hyena(uh_l, l, fwd_l, 1)
        wup = w_up[layer].astype(BF16)
        wout = w_out[layer].astype(BF16)
        xl = _merge(yh_l, yl_rm[0], yl_rm[1], um_l, xl, ml_[:, 2], wup, wout, min(512, l))
        if not last:
            uh_cb = uh_c.reshape(3 * HY_W, b, lc).transpose(1, 0, 2)
            yh_c = hyena(uh_cb, lc, fwd_c, b)
            xc = _merge(yh_c, yc_rm[0], yc_rm[1], um_c, xc, mc_[:, 2], wup, wout, min(512, lc))

        j = layer // 2
        gffn = g_ffn[layer][None, :]
        gfin = g_final[None, :] if last else None
        if layer % 2 == 0:
            w1 = ffn_w1[j][None].astype(BF16)
            w3 = ffn_w3[j][None].astype(BF16)
            w2 = ffn_w2[j][None].astype(BF16)
            router = None
        else:
            w1 = moe_w1[j].astype(BF16)
            w3 = moe_w3[j].astype(BF16)
            w2 = moe_w2[j].astype(BF16)
            wr = jnp.zeros((d, LANES), F32).at[:, :N_EXPERTS].set(moe_router[j])
            br = jnp.zeros((1, LANES), F32).at[0, :N_EXPERTS].set(moe_router_b[j])
            router = (wr, br)
        ff = w1.shape[2]
        tf = ff // 2 if (ff // 2) % LANES == 0 else ff
        xl = _ffn(xl, gffn, ml_[:, 3], ml_[:, 4], ml_[:, 5], w1, w3, w2, router, gfin, min(512, l), tf)
        if not last:
            xcf = _ffn(xc.reshape(1, b * lc, d), gffn, mc_[:, 3], mc_[:, 4], mc_[:, 5], w1, w3, w2,
                       router, None, min(512, b * lc), tf)
            xc = xcf.reshape(b, lc, d)
    if depth == 0:
        raise ValueError("depth must be positive")
    return xl
```

```python
import functools
import math

import jax
import jax.numpy as jnp
from jax import lax
from jax.experimental import pallas as pl
from jax.experimental.pallas import tpu as pltpu

F32 = jnp.float32
BF16 = jnp.bfloat16
EPS = 1e-6

GRID_W = 64
HY_W = 384
RET_HEADS, RET_DH = 6, 64
RET_W = RET_HEADS * RET_DH
ML_HEADS, ML_DH = 4, 64
ML_W = ML_HEADS * ML_DH
HY_EMB = 33
ROPE_BASE = 10000.0
N_EXPERTS = 8
LANES = 128

GATE_OFF = 0
RETM_OFF = 3 * 1024
MLM_OFF = RETM_OFF + 4 * RET_W
MLG_M_OFF = MLM_OFF + 4 * ML_W
N_MAIN = MLG_M_OFF + LANES
N_TILE = 1152
CHUNK = 256

MOE_ROWS = 1024

VMEM_LIMIT = 56 * 1024 * 1024


def _ff_tile(ff):
    return ff // 2 if (ff // 2) % LANES == 0 else ff


def _cp(sem, vmem=None):
    return pltpu.CompilerParams(dimension_semantics=sem, vmem_limit_bytes=vmem)


def _split3(a):
    hi = a.astype(BF16)
    r1 = a - hi.astype(F32)
    mid = r1.astype(BF16)
    lo = (r1 - mid.astype(F32)).astype(BF16)
    return hi, mid, lo


def _dot(a, b):
    return jnp.dot(a, b, preferred_element_type=F32)


def _dot_nt(a, b):
    return lax.dot_general(a, b, (((1,), (1,)), ((), ())), preferred_element_type=F32)


def _dot3(a, b):
    ah, am, al = _split3(a)
    bh, bm, bl = _split3(b)
    return (_dot(ah, bh) + (_dot(ah, bm) + _dot(am, bh))
            + (_dot(am, bm) + _dot(ah, bl) + _dot(al, bh)))


def _sigmoid(x):
    return 1.0 / (1.0 + jnp.exp(-x))


def _silu(x):
    return x * _sigmoid(x)


def _log_sigmoid(x):
    return jnp.minimum(x, 0.0) - jnp.log(1.0 + jnp.exp(-jnp.abs(x)))


def _mod_kernel(c_ref, w_ref, b_ref, o_ref):
    s = _silu(c_ref[...])
    o_ref[0] = _dot3(s, w_ref[0]) + b_ref[0]


def _mod(c_all, w_mod, b_mod):
    depth, d, n = w_mod.shape
    mp = c_all.shape[0]
    tn = 1536
    return pl.pallas_call(
        _mod_kernel,
        grid=(depth, n // tn),
        in_specs=[pl.BlockSpec((mp, d), lambda l, j: (0, 0)),
                  pl.BlockSpec((1, d, tn), lambda l, j: (l, 0, j)),
                  pl.BlockSpec((1, 1, tn), lambda l, j: (l, 0, j))],
        out_specs=pl.BlockSpec((1, mp, tn), lambda l, j: (l, 0, j)),
        out_shape=jax.ShapeDtypeStruct((depth, mp, n), F32),
        compiler_params=_cp(("parallel", "parallel"), VMEM_LIMIT),
        name="mod",
    )(c_all, w_mod, b_mod.reshape(depth, 1, n))


def _norm_mod(x, g, shift, scale):
    y = x * lax.rsqrt(jnp.mean(x * x, axis=-1, keepdims=True) + EPS)
    return (y * g) * (1.0 + scale) + shift


def _inproj_kernel(x_ref, g_ref, sh_ref, sc_ref, wm_ref, bm_ref, wh_ref, bh_ref,
                   um_ref, uh_ref, h_scr):
    n = pl.program_id(2)

    @pl.when(n == 0)
    def _():
        h = _norm_mod(x_ref[0], g_ref[...], sh_ref[0], sc_ref[0]).astype(BF16)
        h_scr[...] = h
        uh_ref[0] = _dot_nt(wh_ref[...], h) + bh_ref[...]

    um_ref[0] = _dot(h_scr[...], wm_ref[...]) + bm_ref[...]


def _in_proj(x, g, shift, scale, w_main, b_main, w_hyt, b_hy, tm):
    bx, lx, d = x.shape
    per_batch_mod = shift.shape[0] > 1
    mi = (lambda b, i, n: (b, 0, 0)) if per_batch_mod else (lambda b, i, n: (0, 0, 0))
    nh = w_hyt.shape[0]
    return pl.pallas_call(
        _inproj_kernel,
        grid=(bx, lx // tm, N_MAIN // N_TILE),
        in_specs=[pl.BlockSpec((1, tm, d), lambda b, i, n: (b, i, 0)),
                  pl.BlockSpec((1, d), lambda b, i, n: (0, 0)),
                  pl.BlockSpec((1, 1, d), mi),
                  pl.BlockSpec((1, 1, d), mi),
                  pl.BlockSpec((d, N_TILE), lambda b, i, n: (0, n)),
                  pl.BlockSpec((1, N_TILE), lambda b, i, n: (0, n)),
                  pl.BlockSpec((nh, d), lambda b, i, n: (0, 0)),
                  pl.BlockSpec((nh, 1), lambda b, i, n: (0, 0))],
        out_specs=[pl.BlockSpec((1, tm, N_TILE), lambda b, i, n: (b, i, n)),
                   pl.BlockSpec((1, nh, tm), lambda b, i, n: (b, 0, i))],
        out_shape=[jax.ShapeDtypeStruct((bx, lx, N_MAIN), F32),
                   jax.ShapeDtypeStruct((bx, nh, lx), F32)],
        scratch_shapes=[pltpu.VMEM((tm, d), BF16)],
        compiler_params=_cp(("parallel", "parallel", "arbitrary"), VMEM_LIMIT),
        name="in_proj",
    )(x, g, shift, scale, w_main, b_main, w_hyt, b_hy)


def _dft_kernel(h_ref, f_ref, o_ref):
    hh, hm, hl = _split3(h_ref[...])
    f = f_ref[0]
    o_ref[0] = _dot(hh, f) + _dot(hm, f) + _dot(hl, f)


def _dft(h_rows, fwd):
    r, l = h_rows.shape
    nf, _, tw = fwd.shape
    return pl.pallas_call(
        _dft_kernel,
        grid=(nf,),
        in_specs=[pl.BlockSpec((r, l), lambda f: (0, 0)),
                  pl.BlockSpec((1, l, tw), lambda f: (f, 0, 0))],
        out_specs=pl.BlockSpec((1, r, tw), lambda f: (f, 0, 0)),
        out_shape=jax.ShapeDtypeStruct((nf, r, tw), F32),
        compiler_params=_cp(("parallel",), VMEM_LIMIT),
        name="dft",
    )(h_rows, fwd)


def _short_conv_lanes(u, w):
    l = u.shape[1]
    t = lax.broadcasted_iota(jnp.int32, u.shape, 1)
    prev = jnp.where(t == 0, 0.0, pltpu.roll(u, 1, 1))
    nxt = jnp.where(t == l - 1, 0.0, pltpu.roll(u, l - 1, 1))
    return prev * w[:, 0:1] + u * w[:, 1:2] + nxt * w[:, 2:3]


def _hyena_kernel(in_ref, gate_ref, wi_ref, wg_ref, skip_ref, f_ref, coef_ref, o_ref,
                  inf_scr, inb_scr, y_scr, acc_scr, *, conv_in, bt, cw, tf):
    f = pl.program_id(1)
    nf = pl.num_programs(1)

    @pl.when(f == 0)
    def _():
        for i in range(bt):
            u = in_ref[i]
            if conv_in:
                u = _short_conv_lanes(u, wi_ref[...])
            inf_scr[i * cw:(i + 1) * cw, :] = u
            inb_scr[i * cw:(i + 1) * cw, :] = u.astype(BF16)
        acc_scr[...] = jnp.zeros_like(acc_scr)

    fw = f_ref[0]
    spec = _dot(inb_scr[...], fw)
    ca, cb, cc, cd = coef_ref[0, 0], coef_ref[0, 1], coef_ref[0, 2], coef_ref[0, 3]
    for i in range(bt):
        ur = spec[i * cw:(i + 1) * cw, :tf]
        ui = spec[i * cw:(i + 1) * cw, tf:]
        y_scr[i * cw:(i + 1) * cw, :tf] = (ur * ca + ui * cb).astype(BF16)
        y_scr[i * cw:(i + 1) * cw, tf:] = (ur * cc + ui * cd).astype(BF16)
    acc_scr[...] += _dot_nt(y_scr[...], fw)

    @pl.when(f == nf - 1)
    def _():
        for i in range(bt):
            gate = _short_conv_lanes(gate_ref[i], wg_ref[...])
            z = acc_scr[i * cw:(i + 1) * cw, :] + skip_ref[...] * inf_scr[i * cw:(i + 1) * cw, :]
            o_ref[i] = gate * z


def _hyena_conv(src, src_blk, gate_src, gate_blk, w_conv_t, skip, fwd, coef, conv_in, bt):
    b, _, l = src.shape
    cw = HY_W
    nf, _, tw = fwd.shape
    tf = tw // 2
    kern = functools.partial(_hyena_kernel, conv_in=conv_in, bt=bt, cw=cw, tf=tf)
    return pl.pallas_call(
        kern,
        grid=(b // bt, nf),
        in_specs=[pl.BlockSpec((bt, cw, l), lambda i, f: (i, src_blk, 0)),
                  pl.BlockSpec((bt, cw, l), lambda i, f: (i, gate_blk, 0)),
                  pl.BlockSpec((cw, 3), lambda i, f: (src_blk if conv_in else 0, 0)),
                  pl.BlockSpec((cw, 3), lambda i, f: (gate_blk, 0)),
                  pl.BlockSpec((cw, 1), lambda i, f: (0, 0)),
                  pl.BlockSpec((1, l, tw), lambda i, f: (f, 0, 0)),
                  pl.BlockSpec((1, 4, cw, tf), lambda i, f: (f, 0, 0, 0))],
        out_specs=pl.BlockSpec((bt, cw, l), lambda i, f: (i, 0, 0)),
        out_shape=jax.ShapeDtypeStruct((b, cw, l), F32),
        scratch_shapes=[pltpu.VMEM((bt * cw, l), F32),
                        pltpu.VMEM((bt * cw, l), BF16),
                        pltpu.VMEM((bt * cw, tw), BF16),
                        pltpu.VMEM((bt * cw, l), F32)],
        compiler_params=_cp(("parallel", "arbitrary"), VMEM_LIMIT),
        name="hyena_conv",
    )(src, gate_src, w_conv_t, w_conv_t, skip, fwd, coef)


def _dft_matrix(l, tf):
    n = jnp.arange(l, dtype=jnp.int32)[:, None]
    k = jnp.arange(l, dtype=jnp.int32)[None, :]
    ang = ((n * k) % (2 * l)).astype(F32) * (math.pi / l)
    re = jnp.cos(ang)
    im = -jnp.sin(ang)
    nyq = jnp.where(n % 2 == 0, 1.0, -1.0).astype(F32)
    im = jnp.where(k == 0, nyq, im)
    nf = l // tf
    re = re.reshape(l, nf, tf)
    im = im.reshape(l, nf, tf)
    return jnp.concatenate([re, im], axis=-1).transpose(1, 0, 2).astype(BF16)


def _hyena_filters(length, f1, fb1, f2, fb2, f3, decay):
    hp = lax.Precision.HIGHEST
    pos = jnp.arange(length, dtype=F32)
    t = pos / max(length - 1, 1)
    n_bands = (HY_EMB - 1) // 2
    bands = jnp.linspace(1e-4, n_bands - 1, n_bands, dtype=F32)
    z = (2.0 * math.pi * pos / length)[:, None] * bands[None, :]
    feats = jnp.concatenate([t[:, None], jnp.cos(z), -jnp.sin(z)], axis=-1)
    hid = jnp.sin(jnp.dot(feats, f1, precision=hp) + fb1)
    hid = jnp.sin(jnp.dot(hid, f2, precision=hp) + fb2)
    h = jnp.dot(hid, f3, precision=hp)
    window = jnp.exp(-t[:, None] * jnp.abs(decay)[None, :])
    return h * window


def _filter_coefs(filt, fwd):
    l = filt.shape[0]
    nf, _, tw = fwd.shape
    tf = tw // 2
    rows = filt.T
    is_bwd = (jnp.arange(rows.shape[0]) // HY_W) % 2 == 1
    first = jnp.arange(l)[None, :] == 0
    rows = jnp.where(is_bwd[:, None] & first, 0.0, rows)
    spec = _dft(rows, fwd)
    spec = spec.reshape(nf, 2, 2, HY_W, 2, tf)
    sf, sb = spec[:, :, 0], spec[:, :, 1]
    kre = sf[..., 0, :] + sb[..., 0, :]
    kim = sf[..., 1, :] - sb[..., 1, :]
    knyq = sf[..., 1, :] + sb[..., 1, :]
    kbin = (jnp.arange(nf)[:, None] * tf + jnp.arange(tf)[None, :])
    is0 = (kbin == 0)[:, None, None, :]
    wgt = jnp.where(is0, 1.0, 2.0) / (2.0 * l)
    ca = kre * wgt
    cb = jnp.where(is0, 0.0, -kim) * wgt
    cc = jnp.where(is0, 0.0, kim) * wgt
    cd = jnp.where(is0, knyq, kre) * wgt
    coef = jnp.stack([ca, cb, cc, cd], axis=2)
    return coef.transpose(1, 0, 2, 3, 4)


def _mlqk_kernel(u_ref, w_ref, o_ref):
    u = u_ref[0]
    l = u.shape[0]
    t = lax.broadcasted_iota(jnp.int32, u.shape, 0)
    prev = jnp.where(t == 0, 0.0, pltpu.roll(u, 1, 0))
    nxt = jnp.where(t == l - 1, 0.0, pltpu.roll(u, l - 1, 0))
    y = _silu(prev * w_ref[0:1, :] + u * w_ref[1:2, :] + nxt * w_ref[2:3, :])
    lane = lax.broadcasted_iota(jnp.int32, u.shape, 1)
    o_ref[0] = jnp.where(lane >= ML_W, y * (ML_DH ** -0.5), y).astype(BF16)


def _mlqk(u_main, ml_conv):
    b, l, _ = u_main.shape
    w = 2 * ML_W
    return pl.pallas_call(
        _mlqk_kernel,
        grid=(b,),
        in_specs=[pl.BlockSpec((1, l, w), lambda i: (i, 0, MLM_OFF // w)),
                  pl.BlockSpec((3, w), lambda i: (0, 0))],
        out_specs=pl.BlockSpec((1, l, w), lambda i: (i, 0, 0)),
        out_shape=jax.ShapeDtypeStruct((b, l, w), BF16),
        compiler_params=_cp(("parallel",), VMEM_LIMIT),
        name="mlqk",
    )(u_main, ml_conv)


def _head_norm_pair(xb, lo):
    def halves(v):
        s_all = jnp.sum(v, axis=1, keepdims=True)
        s_lo = jnp.sum(jnp.where(lo, v, 0.0), axis=1, keepdims=True)
        return jnp.where(lo, s_lo, s_all - s_lo) * (1.0 / 64.0)
    xc = xb - halves(xb)
    return xc * lax.rsqrt(halves(xc * xc) + EPS)


def _scan_kernel(*refs, reverse, use_rope, c):
    it = iter(refs)
    lg_ref = next(it)
    ret_ref = next(it)
    if use_rope:
        cos_ref, sa_ref, sb_ref = next(it), next(it), next(it)
    mlqk_ref = next(it)
    mlvo_ref = next(it)
    mlg_ref = next(it)
    gb_ref = next(it)
    rs0_ref, mc0_ref, mn0_ref, mm0_ref = next(it), next(it), next(it), next(it)
    if reverse:
        of_ref, hf_ref = next(it), next(it)
    o_ref, h_ref = next(it), next(it)
    rs_out, mc_out, mn_out, mm_out = next(it), next(it), next(it), next(it)
    dec_scr, qd_scr, kd_scr, tri_scr, rs_scr, mc_scr, mn_scr, mm_scr = (
        next(it), next(it), next(it), next(it), next(it), next(it), next(it), next(it))

    ci = pl.program_id(1)
    nc = pl.num_programs(1)
    d = 1 if reverse else 0

    tt = lax.broadcasted_iota(jnp.int32, (c, c), 0)
    ss = lax.broadcasted_iota(jnp.int32, (c, c), 1)
    rel = (ss - tt) if reverse else (tt - ss)
    valid = rel >= 0
    lane = lax.broadcasted_iota(jnp.int32, (c, LANES), 1)
    lo = lane < 64
    trow = lax.broadcasted_iota(jnp.int32, (c, LANES), 0)
    r128 = lax.broadcasted_iota(jnp.int32, (LANES, LANES), 0)
    c128 = lax.broadcasted_iota(jnp.int32, (LANES, LANES), 1)
    bdiag = (r128 < 64) == (c128 < 64)
    lo_row = lax.broadcasted_iota(jnp.int32, (1, LANES), 1) < 64

    @pl.when(ci == 0)
    def _():
        relf = jnp.where(valid, rel, 0).astype(F32)
        qpow = ((c - trow) if reverse else (trow + 1)).astype(F32)
        kpow = (trow if reverse else (c - 1 - trow)).astype(F32)
        for h in range(RET_HEADS):
            dec_scr[h] = jnp.where(valid, jnp.exp(relf * lg_ref[h]), 0.0)
        for p in range(RET_HEADS // 2):
            lg_l = jnp.where(lo, lg_ref[2 * p], lg_ref[2 * p + 1])
            qd_scr[p] = jnp.exp(qpow * lg_l)
            kd_scr[p] = jnp.exp(kpow * lg_l)
        tri_scr[...] = jnp.where(valid, 1.0, 0.0).astype(BF16)
        rs_scr[...] = rs0_ref[0]
        mc_scr[...] = mc0_ref[0]
        mn_scr[...] = mn0_ref[0]
        mm_scr[...] = mm0_ref[0]

    u = ret_ref[0]
    for p in range(RET_HEADS // 2):
        sl = slice(p * LANES, (p + 1) * LANES)
        q = u[:, p * LANES:(p + 1) * LANES] * (RET_DH ** -0.5)
        k = u[:, RET_W + p * LANES:RET_W + (p + 1) * LANES]
        v = u[:, 2 * RET_W + p * LANES:2 * RET_W + (p + 1) * LANES].astype(BF16)
        if use_rope:
            cs, sa, sb = cos_ref[:, sl], sa_ref[:, sl], sb_ref[:, sl]
            q = q * cs + pltpu.roll(q, LANES - 16, 1) * sa + pltpu.roll(q, 16, 1) * sb
            k = k * cs + pltpu.roll(k, LANES - 16, 1) * sa + pltpu.roll(k, 16, 1) * sb
        qb = q.astype(BF16)
        kb = k.astype(BF16)
        outs = []
        for hh in range(2):
            qm = jnp.where(lo == (hh == 0), qb, jnp.zeros_like(qb))
            s = _dot_nt(qm, kb) * dec_scr[2 * p + hh]
            outs.append(_dot(s.astype(BF16), v))
        o_intra = jnp.where(lo, outs[0], outs[1])
        st = rs_scr[p]
        o_inter = _dot(qb, st.astype(BF16)) * qd_scr[p]
        o_new = o_intra + o_inter
        kw_t = (k * kd_scr[p]).T.astype(BF16)
        upd = _dot(kw_t, v)
        cdec = jnp.exp(jnp.where(lo_row, lg_ref[2 * p], lg_ref[2 * p + 1]) * float(c))
        rs_scr[p] = st * cdec + jnp.where(bdiag, upd, 0.0)
        if reverse:
            o_tot = of_ref[0, :, sl] + o_new
            g = u[:, 3 * RET_W + p * LANES:3 * RET_W + (p + 1) * LANES]
            o_ref[0, :, sl] = _head_norm_pair(o_tot, lo) * _silu(g)
        else:
            o_ref[0, :, sl] = o_new

    gates = mlg_ref[0] + gb_ref[...]
    lf = _log_sigmoid(gates)
    tri = tri_scr[...]
    l_hi, l_mid, l_lo = _split3(lf)
    cum = _dot(tri, l_hi) + _dot(tri, l_mid) + _dot(tri, l_lo)
    cum_t = cum.T
    gates_t = gates.T
    last = 0 if reverse else c - 1
    qk = mlqk_ref[0]
    vo = mlvo_ref[0]
    for p in range(ML_HEADS // 2):
        sl = slice(p * LANES, (p + 1) * LANES)
        qb = qk[:, p * LANES:(p + 1) * LANES]
        kb = qk[:, ML_W + p * LANES:ML_W + (p + 1) * LANES]
        v = vo[:, p * LANES:(p + 1) * LANES].astype(BF16)
        cst = mc_scr[p]
        nrow = mn_scr[p]
        inter_num = _dot(qb, cst.astype(BF16))
        qf = qb.astype(F32)
        kf = kb.astype(F32)
        qn = qf * nrow
        nums, dens, a_l, m_l, wk_l, as_l = [], [], [], [], [], []
        for hh in range(2):
            h = 2 * p + hh
            ic = d * 8 + h
            fc = d * 8 + 4 + h
            b_col = cum[:, fc:fc + 1]
            b_row = cum_t[fc:fc + 1, :]
            i_row = gates_t[ic:ic + 1, :]
            i_col = gates[:, ic:ic + 1]
            m_prev = mm_scr[h:h + 1, 0:1]
            dlog = jnp.where(valid, b_col - b_row + i_row, -jnp.inf)
            inter = b_col + m_prev
            m_t = jnp.maximum(inter, jnp.max(dlog, axis=1, keepdims=True))
            w = jnp.exp(dlog - m_t)
            a = jnp.exp(inter - m_t)
            qm = jnp.where(lo == (hh == 0), qb, jnp.zeros_like(qb))
            s = _dot_nt(qm, kb) * w
            nums.append(_dot(s.astype(BF16), v))
            qn_h = jnp.sum(jnp.where(lo == (hh == 0), qn, 0.0), axis=1, keepdims=True)
            dens.append(jnp.sum(s, axis=1, keepdims=True) + a * qn_h)
            a_l.append(a)
            m_l.append(m_t)
            total = cum[last:last + 1, fc:fc + 1]
            wlog = total - b_col + i_col
            m_new = jnp.maximum(total + m_prev, jnp.max(wlog, axis=0, keepdims=True))
            wk_l.append(jnp.exp(wlog - m_new))
            as_l.append(jnp.exp(total + m_prev - m_new))
            mm_scr[h:h + 1, :] = jnp.broadcast_to(m_new, (1, LANES))
        a_lanes = jnp.where(lo, a_l[0], a_l[1])
        num = jnp.where(lo, nums[0], nums[1]) + a_lanes * inter_num
        den = jnp.where(lo, dens[0], dens[1])
        m_lanes = jnp.where(lo, m_l[0], m_l[1])
        h_new = num / jnp.maximum(jnp.abs(den), jnp.exp(-m_lanes))
        kw = kf * jnp.where(lo, wk_l[0], wk_l[1])
        as_row = jnp.where(lo_row, as_l[0], as_l[1])
        upd = _dot(kw.T.astype(BF16), v)
        mc_scr[p] = cst * as_row + jnp.where(bdiag, upd, 0.0)
        mn_scr[p] = nrow * as_row + jnp.sum(kw, axis=0, keepdims=True)
        if reverse:
            h_tot = hf_ref[0, :, sl] + h_new
            og = vo[:, ML_W + p * LANES:ML_W + (p + 1) * LANES]
            h_ref[0, :, sl] = _sigmoid(og) * _head_norm_pair(h_tot, lo)
        else:
            h_ref[0, :, sl] = h_new

    @pl.when(ci == nc - 1)
    def _():
        rs_out[0] = rs_scr[...]
        mc_out[0] = mc_scr[...]
        mn_out[0] = mn_scr[...]
        mm_out[0] = mm_scr[...]


def _scan(u_main, mlqk, rope, lg, gate_bias, states, prev, reverse):
    b, l, _ = u_main.shape
    c = min(CHUNK, l)
    nc = l // c
    use_rope = rope is not None
    cix = (lambda i: nc - 1 - i) if reverse else (lambda i: i)
    in_specs = [pl.BlockSpec(memory_space=pltpu.SMEM),
                pl.BlockSpec((1, c, 4 * RET_W), lambda bi, i: (bi, cix(i), RETM_OFF // (4 * RET_W)))]
    args = [lg, u_main]
    if use_rope:
        in_specs += [pl.BlockSpec((c, RET_W), lambda bi, i: (cix(i), 0))] * 3
        args += list(rope)
    in_specs += [pl.BlockSpec((1, c, 2 * ML_W), lambda bi, i: (bi, cix(i), 0)),
                 pl.BlockSpec((1, c, 2 * ML_W), lambda bi, i: (bi, cix(i), MLM_OFF // (2 * ML_W) + 1)),
                 pl.BlockSpec((1, c, LANES), lambda bi, i: (bi, cix(i), MLG_M_OFF // LANES)),
                 pl.BlockSpec((1, LANES), lambda bi, i: (0, 0)),
                 pl.BlockSpec((1, 3, LANES, LANES), lambda bi, i: (bi, 0, 0, 0)),
                 pl.BlockSpec((1, 2, LANES, LANES), lambda bi, i: (bi, 0, 0, 0)),
                 pl.BlockSpec((1, 2, 1, LANES), lambda bi, i: (bi, 0, 0, 0)),
                 pl.BlockSpec((1, 8, LANES), lambda bi, i: (bi, 0, 0))]
    args += [mlqk, u_main, u_main, gate_bias, *states]
    if reverse:
        in_specs += [pl.BlockSpec((1, c, RET_W), lambda bi, i: (bi, cix(i), 0)),
                     pl.BlockSpec((1, c, ML_W), lambda bi, i: (bi, cix(i), 0))]
        args += list(prev)
    out_specs = [pl.BlockSpec((1, c, RET_W), lambda bi, i: (bi, cix(i), 0)),
                 pl.BlockSpec((1, c, ML_W), lambda bi, i: (bi, cix(i), 0)),
                 pl.BlockSpec((1, 3, LANES, LANES), lambda bi, i: (bi, 0, 0, 0)),
                 pl.BlockSpec((1, 2, LANES, LANES), lambda bi, i: (bi, 0, 0, 0)),
                 pl.BlockSpec((1, 2, 1, LANES), lambda bi, i: (bi, 0, 0, 0)),
                 pl.BlockSpec((1, 8, LANES), lambda bi, i: (bi, 0, 0))]
    out_shape = [jax.ShapeDtypeStruct((b, l, RET_W), F32),
                 jax.ShapeDtypeStruct((b, l, ML_W), F32),
                 jax.ShapeDtypeStruct((b, 3, LANES, LANES), F32),
                 jax.ShapeDtypeStruct((b, 2, LANES, LANES), F32),
                 jax.ShapeDtypeStruct((b, 2, 1, LANES), F32),
                 jax.ShapeDtypeStruct((b, 8, LANES), F32)]
    scratch = [pltpu.VMEM((RET_HEADS, c, c), F32),
               pltpu.VMEM((3, c, LANES), F32),
               pltpu.VMEM((3, c, LANES), F32),
               pltpu.VMEM((c, c), BF16),
               pltpu.VMEM((3, LANES, LANES), F32),
               pltpu.VMEM((2, LANES, LANES), F32),
               pltpu.VMEM((2, 1, LANES), F32),
               pltpu.VMEM((8, LANES), F32)]
    kern = functools.partial(_scan_kernel, reverse=reverse, use_rope=use_rope, c=c)
    outs = pl.pallas_call(
        kern, grid=(b, nc), in_specs=in_specs, out_specs=out_specs, out_shape=out_shape,
        scratch_shapes=scratch,
        compiler_params=_cp(("parallel", "arbitrary"), VMEM_LIMIT),
        name="scan_bwd" if reverse else "scan_fwd",
    )(*args)
    return (outs[0], outs[1]), tuple(outs[2:])


def _merge_kernel(yh_ref, yr_ref, ym_ref, gt_ref, x_ref, mg_ref, wa_ref, wb_ref, wc_ref, wo_ref, o_ref):
    d = x_ref.shape[2]
    g = gt_ref[0]
    yh = yh_ref[0].T.astype(BF16)
    acc = _sigmoid(g[:, 0:d]) * _dot(yh, wa_ref[...])
    acc += _sigmoid(g[:, d:2 * d]) * _dot(yr_ref[0].astype(BF16), wb_ref[...])
    acc += _sigmoid(g[:, 2 * d:3 * d]) * _dot(ym_ref[0].astype(BF16), wc_ref[...])
    y = _dot(acc.astype(BF16), wo_ref[...])
    o_ref[0] = x_ref[0] + mg_ref[0] * y


def _merge(y_hyt, y_ret, y_ml, u_main, x, mgate, w_up, w_out, tm):
    b, l, d = x.shape
    per_batch_mod = mgate.shape[0] > 1
    mi = (lambda bi, i: (bi, 0, 0)) if per_batch_mod else (lambda bi, i: (0, 0, 0))
    wa, wb, wc = w_up[:HY_W], w_up[HY_W:HY_W + RET_W], w_up[HY_W + RET_W:]
    const = lambda bi, i: (0, 0)
    return pl.pallas_call(
        _merge_kernel,
        grid=(b, l // tm),
        in_specs=[pl.BlockSpec((1, HY_W, tm), lambda bi, i: (bi, 0, i)),
                  pl.BlockSpec((1, tm, RET_W), lambda bi, i: (bi, i, 0)),
                  pl.BlockSpec((1, tm, ML_W), lambda bi, i: (bi, i, 0)),
                  pl.BlockSpec((1, tm, 3 * d), lambda bi, i: (bi, i, 0)),
                  pl.BlockSpec((1, tm, d), lambda bi, i: (bi, i, 0)),
                  pl.BlockSpec((1, 1, d), mi),
                  pl.BlockSpec(wa.shape, const),
                  pl.BlockSpec(wb.shape, const),
                  pl.BlockSpec(wc.shape, const),
                  pl.BlockSpec(w_out.shape, const)],
        out_specs=pl.BlockSpec((1, tm, d), lambda bi, i: (bi, i, 0)),
        out_shape=jax.ShapeDtypeStruct((b, l, d), F32),
        compiler_params=_cp(("parallel", "parallel"), VMEM_LIMIT),
        name="merge",
    )(y_hyt, y_ret, y_ml, u_main, x, mgate, wa, wb, wc, w_out)


def _ffn_kernel(*refs, final):
    it = iter(refs)
    x_ref, g_ref, sh_ref, sc_ref, mg_ref = next(it), next(it), next(it), next(it), next(it)
    w1_ref, w3_ref, w2_ref = next(it), next(it), next(it)
    if final:
        gf_ref = next(it)
    o_ref = next(it)
    h_scr, acc_scr = next(it), next(it)
    f = pl.program_id(2)

    @pl.when(f == 0)
    def _():
        h = _norm_mod(x_ref[0], g_ref[...], sh_ref[0], sc_ref[0])
        h_scr[...] = h.astype(BF16)
        acc_scr[...] = jnp.zeros_like(acc_scr)

    h = h_scr[...]
    act = _silu(_dot(h, w1_ref[...])) * _dot(h, w3_ref[...])
    acc_scr[...] += _dot(act.astype(BF16), w2_ref[...])

    @pl.when(f == pl.num_programs(2) - 1)
    def _():
        y = x_ref[0] + mg_ref[0] * acc_scr[...]
        if final:
            y = y * lax.rsqrt(jnp.mean(y * y, axis=-1, keepdims=True) + EPS) * gf_ref[...]
        o_ref[0] = y


def _ffn(x, g, shift, scale, mgate, w1, w3, w2, g_final, tm, tf):
    b, l, d = x.shape
    ff = w1.shape[1]
    final = g_final is not None
    per_batch_mod = shift.shape[0] > 1
    mi = (lambda bi, i, f: (bi, 0, 0)) if per_batch_mod else (lambda bi, i, f: (0, 0, 0))
    const2 = lambda bi, i, f: (0, 0)
    in_specs = [pl.BlockSpec((1, tm, d), lambda bi, i, f: (bi, i, 0)),
                pl.BlockSpec((1, d), const2),
                pl.BlockSpec((1, 1, d), mi), pl.BlockSpec((1, 1, d), mi), pl.BlockSpec((1, 1, d), mi),
                pl.BlockSpec((d, tf), lambda bi, i, f: (0, f)),
                pl.BlockSpec((d, tf), lambda bi, i, f: (0, f)),
                pl.BlockSpec((tf, d), lambda bi, i, f: (f, 0))]
    args = [x, g, shift, scale, mgate, w1, w3, w2]
    if final:
        in_specs += [pl.BlockSpec((1, d), const2)]
        args += [g_final]
    return pl.pallas_call(
        functools.partial(_ffn_kernel, final=final),
        grid=(b, l // tm, ff // tf),
        in_specs=in_specs,
        out_specs=pl.BlockSpec((1, tm, d), lambda bi, i, f: (bi, i, 0)),
        out_shape=jax.ShapeDtypeStruct((b, l, d), F32),
        scratch_shapes=[pltpu.VMEM((tm, d), BF16), pltpu.VMEM((tm, d), F32)],
        compiler_params=_cp(("parallel", "parallel", "arbitrary"), VMEM_LIMIT),
        name="ffn",
    )(*args)


def _top2(logits):
    lane = lax.broadcasted_iota(jnp.int32, logits.shape, 1)
    neg = jnp.float32(-jnp.inf)
    logits = jnp.where(lane < N_EXPERTS, logits, neg)
    m1 = jnp.max(logits, axis=1, keepdims=True)
    i1 = jnp.min(jnp.where(logits == m1, lane, LANES), axis=1, keepdims=True)
    rest = jnp.where(lane == i1, neg, logits)
    m2 = jnp.max(rest, axis=1, keepdims=True)
    i2 = jnp.min(jnp.where(rest == m2, lane, LANES), axis=1, keepdims=True)
    e2 = jnp.exp(m2 - m1)
    return lane, i1, i2, 1.0 / (1.0 + e2), e2 / (1.0 + e2)


def _route_kernel(x_ref, g_ref, sh_ref, sc_ref, wr_ref, br_ref, h_ref, meta_ref, cnt_ref, cnt_scr):
    first = jnp.logical_and(pl.program_id(0) == 0, pl.program_id(1) == 0)

    @pl.when(first)
    def _():
        cnt_scr[...] = jnp.zeros_like(cnt_scr)

    h = _norm_mod(x_ref[0], g_ref[...], sh_ref[0], sc_ref[0])
    h_ref[0] = h
    tm = h.shape[0]
    lane, i1, i2, w1, w2 = _top2(_dot3(h, wr_ref[...]) + br_ref[...])
    oh1 = (lane == i1).astype(F32)
    oh2 = (lane == i2).astype(F32)
    both = oh1 + oh2
    tt = lax.broadcasted_iota(jnp.int32, (tm, tm), 0)
    ss = lax.broadcasted_iota(jnp.int32, (tm, tm), 1)
    before = jnp.where(tt > ss, 1.0, 0.0).astype(BF16)
    pref = _dot(before, both.astype(BF16)) + cnt_scr[...]
    r1 = jnp.sum(pref * oh1, axis=1, keepdims=True)
    r2 = jnp.sum(pref * oh2, axis=1, keepdims=True)
    cnt_scr[...] += jnp.sum(both, axis=0, keepdims=True)
    cnt_ref[...] = cnt_scr[...]
    vals = (i1.astype(F32), i2.astype(F32), r1, r2, w1, w2)
    meta = jnp.zeros((tm, LANES), F32)
    for j, val in enumerate(vals):
        meta = jnp.where(lane == j, val, meta)
    meta_ref[0] = meta[:, :8]


def _route(x, g, shift, scale, wr, br, tm):
    b, l, d = x.shape
    const2 = lambda bi, i: (0, 0)
    return pl.pallas_call(
        _route_kernel,
        grid=(b, l // tm),
        in_specs=[pl.BlockSpec((1, tm, d), lambda bi, i: (bi, i, 0)),
                  pl.BlockSpec((1, d), const2),
                  pl.BlockSpec((1, 1, d), lambda bi, i: (bi, 0, 0)),
                  pl.BlockSpec((1, 1, d), lambda bi, i: (bi, 0, 0)),
                  pl.BlockSpec((d, LANES), const2),
                  pl.BlockSpec((1, LANES), const2)],
        out_specs=[pl.BlockSpec((1, tm, d), lambda bi, i: (bi, i, 0)),
                   pl.BlockSpec((1, tm, 8), lambda bi, i: (bi, i, 0)),
                   pl.BlockSpec((1, LANES), const2)],
        out_shape=[jax.ShapeDtypeStruct((b, l, d), F32),
                   jax.ShapeDtypeStruct((b, l, 8), F32),
                   jax.ShapeDtypeStruct((1, LANES), F32)],
        scratch_shapes=[pltpu.VMEM((1, LANES), F32)],
        compiler_params=_cp(("arbitrary", "arbitrary"), VMEM_LIMIT),
        name="route",
    )(x, g, shift, scale, wr, br)


def _row_copy(src_ref, src_row, dst_ref, dst_row, sem):
    return pltpu.make_async_copy(src_ref.at[pl.ds(src_row, 1)], dst_ref.at[pl.ds(dst_row, 1)], sem)


def _scatter_kernel(dest_ref, h_ref, init_ref, xs_ref, sem):
    del init_ref
    tm = h_ref.shape[0]

    def issue(r, carry):
        _row_copy(h_ref, r, xs_ref, dest_ref[0, 0, r], sem).start()
        _row_copy(h_ref, r, xs_ref, dest_ref[0, 0, tm + r], sem).start()
        return carry

    lax.fori_loop(0, tm, issue, 0)

    def drain(r, carry):
        _row_copy(h_ref, 0, xs_ref, 0, sem).wait()
        return carry

    lax.fori_loop(0, 2 * tm, drain, 0)


def _scatter_rows(h_flat, dest, n_rows, tm):
    t, d = h_flat.shape
    nt = t // tm
    dest_blk = jnp.concatenate([dest[0].reshape(nt, 1, tm), dest[1].reshape(nt, 1, tm)], axis=2)
    return pl.pallas_call(
        _scatter_kernel,
        grid=(nt,),
        in_specs=[pl.BlockSpec((1, 1, 2 * tm), lambda i: (i, 0, 0), memory_space=pltpu.SMEM),
                  pl.BlockSpec((tm, d), lambda i: (i, 0)),
                  pl.BlockSpec(memory_space=pl.ANY)],
        out_specs=pl.BlockSpec(memory_space=pl.ANY),
        out_shape=jax.ShapeDtypeStruct((n_rows, d), F32),
        scratch_shapes=[pltpu.SemaphoreType.DMA],
        input_output_aliases={2: 0},
        compiler_params=_cp(("arbitrary",), VMEM_LIMIT),
        name="moe_scatter",
    )(dest_blk, h_flat, jnp.zeros((n_rows, d), F32))


def _experts_kernel(te_ref, nv_ref, x_ref, w1_ref, w3_ref, w2_ref, o_ref, h_scr, acc_scr):
    del te_ref
    i = pl.program_id(0)
    f = pl.program_id(1)
    live = i < nv_ref[0]

    @pl.when(jnp.logical_and(live, f == 0))
    def _():
        h_scr[...] = x_ref[...].astype(BF16)
        acc_scr[...] = jnp.zeros_like(acc_scr)

    @pl.when(live)
    def _():
        h = h_scr[...]
        act = _silu(_dot(h, w1_ref[0])) * _dot(h, w3_ref[0])
        acc_scr[...] += _dot(act.astype(BF16), w2_ref[0])

    @pl.when(f == pl.num_programs(1) - 1)
    def _():
        o_ref[...] = jnp.where(live, acc_scr[...], 0.0)


def _experts(xs, tile_expert, n_valid, w1, w3, w2, tm, tf):
    p, d = xs.shape
    ff = w1.shape[2]
    nt = p // tm
    row = lambda i, f, te, nv: (jnp.minimum(i, nv[0] - 1), 0)
    grid_spec = pltpu.PrefetchScalarGridSpec(
        num_scalar_prefetch=2,
        grid=(nt, ff // tf),
        in_specs=[pl.BlockSpec((tm, d), row),
                  pl.BlockSpec((1, d, tf), lambda i, f, te, nv: (te[i], 0, f)),
                  pl.BlockSpec((1, d, tf), lambda i, f, te, nv: (te[i], 0, f)),
                  pl.BlockSpec((1, tf, d), lambda i, f, te, nv: (te[i], f, 0))],
        out_specs=pl.BlockSpec((tm, d), lambda i, f, te, nv: (i, 0)),
        scratch_shapes=[pltpu.VMEM((tm, d), BF16), pltpu.VMEM((tm, d), F32)])
    return pl.pallas_call(
        _experts_kernel,
        grid_spec=grid_spec,
        out_shape=jax.ShapeDtypeStruct((p, d), F32),
        compiler_params=_cp(("arbitrary", "arbitrary"), VMEM_LIMIT),
        name="moe_experts",
    )(tile_expert, n_valid, xs, w1, w3, w2)


def _combine_kernel(dest_ref, meta_ref, x_ref, mg_ref, gf_ref, ys_ref, o_ref, ybuf, sem, *, final):
    tm = x_ref.shape[1]

    def issue(r, carry):
        _row_copy(ys_ref, dest_ref[0, 0, r], ybuf.at[0], r, sem).start()
        _row_copy(ys_ref, dest_ref[0, 0, tm + r], ybuf.at[1], r, sem).start()
        return carry

    lax.fori_loop(0, tm, issue, 0)

    def drain(r, carry):
        _row_copy(ys_ref, 0, ybuf.at[0], 0, sem).wait()
        return carry

    lax.fori_loop(0, 2 * tm, drain, 0)
    meta = meta_ref[0]
    y = x_ref[0] + mg_ref[0] * (meta[:, 4:5] * ybuf[0] + meta[:, 5:6] * ybuf[1])
    if final:
        y = y * lax.rsqrt(jnp.mean(y * y, axis=-1, keepdims=True) + EPS) * gf_ref[...]
    o_ref[0] = y


def _combine(ys, dest, meta, x, mgate, g_final, tm):
    b, l, d = x.shape
    nt = l // tm
    final = g_final is not None
    gf = g_final if final else jnp.ones((1, d), F32)
    dest_blk = jnp.concatenate([dest[0].reshape(b * nt, 1, tm), dest[1].reshape(b * nt, 1, tm)], axis=2)
    return pl.pallas_call(
        functools.partial(_combine_kernel, final=final),
        grid=(b, nt),
        in_specs=[pl.BlockSpec((1, 1, 2 * tm), lambda bi, i: (bi * nt + i, 0, 0), memory_space=pltpu.SMEM),
                  pl.BlockSpec((1, tm, 8), lambda bi, i: (bi, i, 0)),
                  pl.BlockSpec((1, tm, d), lambda bi, i: (bi, i, 0)),
                  pl.BlockSpec((1, 1, d), lambda bi, i: (bi, 0, 0)),
                  pl.BlockSpec((1, d), lambda bi, i: (0, 0)),
                  pl.BlockSpec(memory_space=pl.ANY)],
        out_specs=pl.BlockSpec((1, tm, d), lambda bi, i: (bi, i, 0)),
        out_shape=jax.ShapeDtypeStruct((b, l, d), F32),
        scratch_shapes=[pltpu.VMEM((2, tm, d), F32), pltpu.SemaphoreType.DMA],
        compiler_params=_cp(("arbitrary", "arbitrary"), VMEM_LIMIT),
        name="moe_combine",
    )(dest_blk, meta, x, mgate, gf, ys)


def _moe_ffn(x, g, shift, scale, mgate, wr, br, w1, w3, w2, g_final, tm_e, tf):
    b, l, d = x.shape
    t = b * l
    ne = w1.shape[0]
    h, meta, cnt = _route(x, g, shift, scale, wr, br, min(512, l))
    e_idx = jnp.stack([meta[..., 0], meta[..., 1]]).astype(jnp.int32).reshape(2, t)
    rank = jnp.stack([meta[..., 2], meta[..., 3]]).astype(jnp.int32).reshape(2, t)
    counts = cnt[0, :ne].astype(jnp.int32)
    padded = ((counts + tm_e - 1) // tm_e) * tm_e
    ends = jnp.cumsum(padded)
    dest = (ends - padded)[e_idx] + rank
    n_tiles = (2 * t) // tm_e + ne
    starts = jnp.arange(n_tiles, dtype=jnp.int32) * tm_e
    tile_expert = jnp.minimum(jnp.sum(starts[:, None] >= ends[None, :], axis=1), ne - 1).astype(jnp.int32)
    n_valid = (ends[-1] // tm_e).astype(jnp.int32).reshape(1)
    xs = _scatter_rows(h.reshape(t, d), dest, n_tiles * tm_e, min(256, l))
    ys = _experts(xs, tile_expert, n_valid, w1, w3, w2, tm_e, tf)
    return _combine(ys, dest, meta, x, mgate, g_final, min(256, l))


def _rope_tables(n_lat):
    rows = n_lat // GRID_W
    row_id = jnp.broadcast_to(jnp.arange(rows)[:, None], (rows, GRID_W)).reshape(-1).astype(F32)
    col_id = jnp.broadcast_to(jnp.arange(GRID_W)[None, :], (rows, GRID_W)).reshape(-1).astype(F32)
    q4 = RET_DH // 4
    inv = 1.0 / (ROPE_BASE ** (jnp.arange(q4, dtype=F32) / q4))
    ang_r = row_id[:, None] * inv[None, :]
    ang_c = col_id[:, None] * inv[None, :]
    zeros = jnp.zeros_like(ang_r)
    cos_h = jnp.concatenate([jnp.cos(ang_r)] * 2 + [jnp.cos(ang_c)] * 2, axis=-1)
    sin_a = jnp.concatenate([-jnp.sin(ang_r), zeros, -jnp.sin(ang_c), zeros], axis=-1)
    sin_b = jnp.concatenate([zeros, jnp.sin(ang_r), zeros, jnp.sin(ang_c)], axis=-1)
    tile = lambda t: jnp.tile(t, (1, RET_HEADS))
    return tile(cos_h), tile(sin_a), tile(sin_b)


def _zero_states(b):
    return (jnp.zeros((b, 3, LANES, LANES), F32), jnp.zeros((b, 2, LANES, LANES), F32),
            jnp.zeros((b, 2, 1, LANES), F32), jnp.zeros((b, 8, LANES), F32))


def kernel(x, c, ctx, c_ctx, w_mod, b_mod, g_mix, g_ffn, w_in, b_in, hy_conv, hy_f1, hy_fb1, hy_f2,
           hy_fb2, hy_f3, hy_decay, hy_skip, ret_decay, ml_conv, ml_gate_bias, w_up, w_out, ffn_w1,
           ffn_w3, ffn_w2, moe_router, moe_router_b, moe_w1, moe_w3, moe_w2, g_final):
    b, l, d = x.shape
    lc = ctx.shape[1]
    depth = w_mod.shape[0]
    hy_off, ret_off = 0, 3 * HY_W
    ml_off = ret_off + 4 * RET_W
    mlg_off = ml_off + 4 * ML_W
    gate_off = mlg_off + 4 * ML_HEADS

    mp = ((b + 1 + 7) // 8) * 8
    c_all = jnp.zeros((mp, d), F32).at[:b].set(c).at[b].set(c_ctx)
    mods = _mod(c_all, w_mod, b_mod).reshape(depth, mp, 6, d)

    rope = _rope_tables(l)
    tf_l, tf_c = min(256, l), min(256, lc)
    fwd_l = _dft_matrix(l, tf_l)
    fwd_c = _dft_matrix(lc, tf_c)
    tm_l = min(1024, l)
    tm_c = min(1024, b * lc)

    xl, xc = x, ctx
    for layer in range(depth):
        last = layer == depth - 1
        ml_ = mods[layer, :b][:, :, None, :]
        mc_ = mods[layer, b][None, :, None, :]
        wi = w_in[layer]
        bi = b_in[layer]
        w_main = jnp.concatenate(
            [wi[:, gate_off:], wi[:, ret_off:ml_off], wi[:, ml_off:mlg_off], wi[:, mlg_off:gate_off],
             jnp.zeros((d, LANES - 4 * ML_HEADS), F32)], axis=1).astype(BF16)
        b_main = jnp.concatenate(
            [bi[gate_off:], bi[ret_off:ml_off], bi[ml_off:mlg_off], bi[mlg_off:gate_off],
             jnp.zeros((LANES - 4 * ML_HEADS,), F32)])[None, :]
        w_hyt = wi[:, hy_off:ret_off].T.astype(BF16)
        b_hy = bi[hy_off:ret_off][:, None]
        gmix = g_mix[layer][None, :]

        um_l, uh_l = _in_proj(xl, gmix, ml_[:, 0], ml_[:, 1], w_main, b_main, w_hyt, b_hy, tm_l)
        xc_flat = xc.reshape(1, b * lc, d)
        um_c, uh_c = _in_proj(xc_flat, gmix, mc_[:, 0], mc_[:, 1], w_main, b_main, w_hyt, b_hy, tm_c)
        um_c = um_c.reshape(b, lc, N_MAIN)

        lg = jax.nn.log_sigmoid(ret_decay[layer].astype(F32))
        gbias = jnp.zeros((1, LANES), F32).at[0, :4 * ML_HEADS].set(ml_gate_bias[layer].reshape(-1))
        qk_c = _mlqk(um_c, ml_conv[layer])
        qk_l = _mlqk(um_l, ml_conv[layer])
        z0 = _zero_states(b)
        pf_c, sf = _scan(um_c, qk_c, None, lg[0], gbias, z0, None, False)
        yc_rm, sb = _scan(um_c, qk_c, None, lg[1], gbias, z0, pf_c, True)
        pf_l, _ = _scan(um_l, qk_l, rope, lg[0], gbias, sf, None, False)
        yl_rm, _ = _scan(um_l, qk_l, rope, lg[1], gbias, sb, pf_l, True)

        w_conv_t = hy_conv[layer].T
        skip = hy_skip[layer][:, :, None]

        def hyena(uh, length, fwd, bt):
            filt = _hyena_filters(length, hy_f1[layer], hy_fb1[layer], hy_f2[layer], hy_fb2[layer],
                                  hy_f3[layer], hy_decay[layer])
            coef = _filter_coefs(filt, fwd)
            z1 = _hyena_conv(uh, 0, uh, 1, w_conv_t, skip[0], fwd, coef[0], True, bt)
            return _hyena_conv(z1, 0, uh, 2, w_conv_t, skip[1], fwd, coef[1], False, bt)

        yh_l = hyena(uh_l, l, fwd_l, 1)
        wup = w_up[layer].astype(BF16)
        wout = w_out[layer].astype(BF16)
        xl = _merge(yh_l, yl_rm[0], yl_rm[1], um_l, xl, ml_[:, 2], wup, wout, min(512, l))
        if not last:
            uh_cb = uh_c.reshape(3 * HY_W, b, lc).transpose(1, 0, 2)
            yh_c = hyena(uh_cb, lc, fwd_c, b)
            xc = _merge(yh_c, yc_rm[0], yc_rm[1], um_c, xc, mc_[:, 2], wup, wout, min(512, lc))

        j = layer // 2
        gffn = g_ffn[layer][None, :]
        gfin = g_final[None, :] if last else None
        xcf = xc.reshape(1, b * lc, d)
        if layer % 2 == 0:
            w1 = ffn_w1[j].astype(BF16)
            w3 = ffn_w3[j].astype(BF16)
            w2 = ffn_w2[j].astype(BF16)
            tf = _ff_tile(w1.shape[1])
            xl = _ffn(xl, gffn, ml_[:, 3], ml_[:, 4], ml_[:, 5], w1, w3, w2, gfin, min(512, l), tf)
            if not last:
                xcf = _ffn(xcf, gffn, mc_[:, 3], mc_[:, 4], mc_[:, 5], w1, w3, w2, None, min(512, b * lc), tf)
        else:
            w1 = moe_w1[j].astype(BF16)
            w3 = moe_w3[j].astype(BF16)
            w2 = moe_w2[j].astype(BF16)
            wr = jnp.zeros((d, LANES), F32).at[:, :N_EXPERTS].set(moe_router[j])
            br = jnp.zeros((1, LANES), F32).at[0, :N_EXPERTS].set(moe_router_b[j])
            tf = _ff_tile(w1.shape[2])
            xl = _moe_ffn(xl, gffn, ml_[:, 3], ml_[:, 4], ml_[:, 5], wr, br, w1, w3, w2, gfin, MOE_ROWS, tf)
            if not last:
                xcf = _moe_ffn(xcf, gffn, mc_[:, 3], mc_[:, 4], mc_[:, 5], wr, br, w1, w3, w2, None, MOE_ROWS, tf)
        xc = xcf.reshape(b, lc, d)
    if depth == 0:
        raise ValueError("depth must be positive")
    return xl
```

```python
import functools
import math

import jax
import jax.numpy as jnp
from jax import lax
from jax.experimental import pallas as pl
from jax.experimental.pallas import tpu as pltpu

F32 = jnp.float32
BF16 = jnp.bfloat16
EPS = 1e-6

GRID_W = 64
HY_W = 384
RET_HEADS, RET_DH = 6, 64
RET_W = RET_HEADS * RET_DH
ML_HEADS, ML_DH = 4, 64
ML_W = ML_HEADS * ML_DH
HY_EMB = 33
ROPE_BASE = 10000.0
N_EXPERTS = 8
LANES = 128

GATE_OFF = 0
RETM_OFF = 3 * 1024
MLM_OFF = RETM_OFF + 4 * RET_W
N_MAIN = MLM_OFF + 4 * ML_W
N_TILE = 1408
CHUNK = 256

MOE_ROWS = 1024

VMEM_LIMIT = 56 * 1024 * 1024


def _ff_tile(ff):
    return ff // 2 if (ff // 2) % LANES == 0 else ff


def _cp(sem, vmem=None):
    return pltpu.CompilerParams(dimension_semantics=sem, vmem_limit_bytes=vmem)


def _split3(a):
    hi = a.astype(BF16)
    r1 = a - hi.astype(F32)
    mid = r1.astype(BF16)
    lo = (r1 - mid.astype(F32)).astype(BF16)
    return hi, mid, lo


def _dot(a, b):
    return jnp.dot(a, b, preferred_element_type=F32)


def _dot_nt(a, b):
    return lax.dot_general(a, b, (((1,), (1,)), ((), ())), preferred_element_type=F32)


def _dot3(a, b):
    ah, am, al = _split3(a)
    bh, bm, bl = _split3(b)
    return (_dot(ah, bh) + (_dot(ah, bm) + _dot(am, bh))
            + (_dot(am, bm) + _dot(ah, bl) + _dot(al, bh)))


def _sigmoid(x):
    return 1.0 / (1.0 + jnp.exp(-x))


def _silu(x):
    return x * _sigmoid(x)


def _log_sigmoid(x):
    return jnp.minimum(x, 0.0) - jnp.log(1.0 + jnp.exp(-jnp.abs(x)))


def _mod_kernel(c_ref, w_ref, b_ref, o_ref):
    s = _silu(c_ref[...])
    o_ref[0] = _dot3(s, w_ref[0]) + b_ref[0]


def _mod(c_all, w_mod, b_mod):
    depth, d, n = w_mod.shape
    mp = c_all.shape[0]
    tn = 1536
    return pl.pallas_call(
        _mod_kernel,
        grid=(depth, n // tn),
        in_specs=[pl.BlockSpec((mp, d), lambda l, j: (0, 0)),
                  pl.BlockSpec((1, d, tn), lambda l, j: (l, 0, j)),
                  pl.BlockSpec((1, 1, tn), lambda l, j: (l, 0, j))],
        out_specs=pl.BlockSpec((1, mp, tn), lambda l, j: (l, 0, j)),
        out_shape=jax.ShapeDtypeStruct((depth, mp, n), F32),
        compiler_params=_cp(("parallel", "parallel"), VMEM_LIMIT),
        name="mod",
    )(c_all, w_mod, b_mod.reshape(depth, 1, n))


def _norm_mod(x, g, shift, scale):
    y = x * lax.rsqrt(jnp.mean(x * x, axis=-1, keepdims=True) + EPS)
    return (y * g) * (1.0 + scale) + shift


def _inproj_kernel(x_ref, g_ref, sh_ref, sc_ref, wm_ref, bm_ref, wh_ref, bh_ref, wg_ref, bg_ref,
                   um_ref, uh_ref, ug_ref, h_scr):
    n = pl.program_id(2)

    @pl.when(n == 0)
    def _():
        h = _norm_mod(x_ref[0], g_ref[...], sh_ref[0], sc_ref[0]).astype(BF16)
        h_scr[...] = h
        uh_ref[0] = _dot_nt(wh_ref[...], h) + bh_ref[...]
        ug_ref[0] = _dot(h, wg_ref[...]) + bg_ref[...]

    um_ref[0] = (_dot(h_scr[...], wm_ref[...]) + bm_ref[...]).astype(BF16)


def _in_proj(x, g, shift, scale, w_main, b_main, w_hyt, b_hy, w_g, b_g, tm):
    bx, lx, d = x.shape
    per_batch_mod = shift.shape[0] > 1
    mi = (lambda b, i, n: (b, 0, 0)) if per_batch_mod else (lambda b, i, n: (0, 0, 0))
    nh = w_hyt.shape[0]
    return pl.pallas_call(
        _inproj_kernel,
        grid=(bx, lx // tm, N_MAIN // N_TILE),
        in_specs=[pl.BlockSpec((1, tm, d), lambda b, i, n: (b, i, 0)),
                  pl.BlockSpec((1, d), lambda b, i, n: (0, 0)),
                  pl.BlockSpec((1, 1, d), mi),
                  pl.BlockSpec((1, 1, d), mi),
                  pl.BlockSpec((d, N_TILE), lambda b, i, n: (0, n)),
                  pl.BlockSpec((1, N_TILE), lambda b, i, n: (0, n)),
                  pl.BlockSpec((nh, d), lambda b, i, n: (0, 0)),
                  pl.BlockSpec((nh, 1), lambda b, i, n: (0, 0)),
                  pl.BlockSpec((d, LANES), lambda b, i, n: (0, 0)),
                  pl.BlockSpec((1, LANES), lambda b, i, n: (0, 0))],
        out_specs=[pl.BlockSpec((1, tm, N_TILE), lambda b, i, n: (b, i, n)),
                   pl.BlockSpec((1, nh, tm), lambda b, i, n: (b, 0, i)),
                   pl.BlockSpec((1, tm, LANES), lambda b, i, n: (b, i, 0))],
        out_shape=[jax.ShapeDtypeStruct((bx, lx, N_MAIN), BF16),
                   jax.ShapeDtypeStruct((bx, nh, lx), F32),
                   jax.ShapeDtypeStruct((bx, lx, LANES), F32)],
        scratch_shapes=[pltpu.VMEM((tm, d), BF16)],
        compiler_params=_cp(("parallel", "parallel", "arbitrary"), VMEM_LIMIT),
        name="in_proj",
    )(x, g, shift, scale, w_main, b_main, w_hyt, b_hy, w_g, b_g)


def _dft_kernel(h_ref, f_ref, o_ref):
    hh, hm, hl = _split3(h_ref[...])
    f = f_ref[0]
    o_ref[0] = _dot(hh, f) + _dot(hm, f) + _dot(hl, f)


def _dft(h_rows, fwd):
    r, l = h_rows.shape
    nf, _, tw = fwd.shape
    return pl.pallas_call(
        _dft_kernel,
        grid=(nf,),
        in_specs=[pl.BlockSpec((r, l), lambda f: (0, 0)),
                  pl.BlockSpec((1, l, tw), lambda f: (f, 0, 0))],
        out_specs=pl.BlockSpec((1, r, tw), lambda f: (f, 0, 0)),
        out_shape=jax.ShapeDtypeStruct((nf, r, tw), F32),
        compiler_params=_cp(("parallel",), VMEM_LIMIT),
        name="dft",
    )(h_rows, fwd)


def _short_conv_lanes(u, w):
    l = u.shape[1]
    t = lax.broadcasted_iota(jnp.int32, u.shape, 1)
    prev = jnp.where(t == 0, 0.0, pltpu.roll(u, 1, 1))
    nxt = jnp.where(t == l - 1, 0.0, pltpu.roll(u, l - 1, 1))
    return prev * w[:, 0:1] + u * w[:, 1:2] + nxt * w[:, 2:3]


def _hyena_kernel(in_ref, gate_ref, wi_ref, wg_ref, skip_ref, f_ref, coef_ref, o_ref,
                  inf_scr, inb_scr, y_scr, acc_scr, *, conv_in, bt, cw, tf):
    f = pl.program_id(1)
    nf = pl.num_programs(1)

    @pl.when(f == 0)
    def _():
        for i in range(bt):
            u = in_ref[i]
            if conv_in:
                u = _short_conv_lanes(u, wi_ref[...])
            inf_scr[i * cw:(i + 1) * cw, :] = u
            inb_scr[i * cw:(i + 1) * cw, :] = u.astype(BF16)
        acc_scr[...] = jnp.zeros_like(acc_scr)

    fw = f_ref[0]
    spec = _dot(inb_scr[...], fw)
    ca, cb, cc, cd = coef_ref[0, 0], coef_ref[0, 1], coef_ref[0, 2], coef_ref[0, 3]
    for i in range(bt):
        ur = spec[i * cw:(i + 1) * cw, :tf]
        ui = spec[i * cw:(i + 1) * cw, tf:]
        y_scr[i * cw:(i + 1) * cw, :tf] = (ur * ca + ui * cb).astype(BF16)
        y_scr[i * cw:(i + 1) * cw, tf:] = (ur * cc + ui * cd).astype(BF16)
    acc_scr[...] += _dot_nt(y_scr[...], fw)

    @pl.when(f == nf - 1)
    def _():
        for i in range(bt):
            gate = _short_conv_lanes(gate_ref[i], wg_ref[...])
            z = acc_scr[i * cw:(i + 1) * cw, :] + skip_ref[...] * inf_scr[i * cw:(i + 1) * cw, :]
            o_ref[i] = gate * z


def _hyena_conv(src, src_blk, gate_src, gate_blk, w_conv_t, skip, fwd, coef, conv_in, bt):
    b, _, l = src.shape
    cw = HY_W
    nf, _, tw = fwd.shape
    tf = tw // 2
    kern = functools.partial(_hyena_kernel, conv_in=conv_in, bt=bt, cw=cw, tf=tf)
    return pl.pallas_call(
        kern,
        grid=(b // bt, nf),
        in_specs=[pl.BlockSpec((bt, cw, l), lambda i, f: (i, src_blk, 0)),
                  pl.BlockSpec((bt, cw, l), lambda i, f: (i, gate_blk, 0)),
                  pl.BlockSpec((cw, 3), lambda i, f: (src_blk if conv_in else 0, 0)),
                  pl.BlockSpec((cw, 3), lambda i, f: (gate_blk, 0)),
                  pl.BlockSpec((cw, 1), lambda i, f: (0, 0)),
                  pl.BlockSpec((1, l, tw), lambda i, f: (f, 0, 0)),
                  pl.BlockSpec((1, 4, cw, tf), lambda i, f: (f, 0, 0, 0))],
        out_specs=pl.BlockSpec((bt, cw, l), lambda i, f: (i, 0, 0)),
        out_shape=jax.ShapeDtypeStruct((b, cw, l), F32),
        scratch_shapes=[pltpu.VMEM((bt * cw, l), F32),
                        pltpu.VMEM((bt * cw, l), BF16),
                        pltpu.VMEM((bt * cw, tw), BF16),
                        pltpu.VMEM((bt * cw, l), F32)],
        compiler_params=_cp(("parallel", "arbitrary"), VMEM_LIMIT),
        name="hyena_conv",
    )(src, gate_src, w_conv_t, w_conv_t, skip, fwd, coef)


def _dft_matrix(l, tf):
    n = jnp.arange(l, dtype=jnp.int32)[:, None]
    k = jnp.arange(l, dtype=jnp.int32)[None, :]
    ang = ((n * k) % (2 * l)).astype(F32) * (math.pi / l)
    re = jnp.cos(ang)
    im = -jnp.sin(ang)
    nyq = jnp.where(n % 2 == 0, 1.0, -1.0).astype(F32)
    im = jnp.where(k == 0, nyq, im)
    nf = l // tf
    re = re.reshape(l, nf, tf)
    im = im.reshape(l, nf, tf)
    return jnp.concatenate([re, im], axis=-1).transpose(1, 0, 2).astype(BF16)


def _hyena_filters(length, f1, fb1, f2, fb2, f3, decay):
    hp = lax.Precision.HIGHEST
    pos = jnp.arange(length, dtype=F32)
    t = pos / max(length - 1, 1)
    n_bands = (HY_EMB - 1) // 2
    bands = jnp.linspace(1e-4, n_bands - 1, n_bands, dtype=F32)
    z = (2.0 * math.pi * pos / length)[:, None] * bands[None, :]
    feats = jnp.concatenate([t[:, None], jnp.cos(z), -jnp.sin(z)], axis=-1)
    hid = jnp.sin(jnp.dot(feats, f1, precision=hp) + fb1)
    hid = jnp.sin(jnp.dot(hid, f2, precision=hp) + fb2)
    h = jnp.dot(hid, f3, precision=hp)
    window = jnp.exp(-t[:, None] * jnp.abs(decay)[None, :])
    return h * window


def _filter_coefs(filt, fwd):
    l = filt.shape[0]
    nf, _, tw = fwd.shape
    tf = tw // 2
    rows = filt.T
    is_bwd = (jnp.arange(rows.shape[0]) // HY_W) % 2 == 1
    first = jnp.arange(l)[None, :] == 0
    rows = jnp.where(is_bwd[:, None] & first, 0.0, rows)
    spec = _dft(rows, fwd)
    spec = spec.reshape(nf, 2, 2, HY_W, 2, tf)
    sf, sb = spec[:, :, 0], spec[:, :, 1]
    kre = sf[..., 0, :] + sb[..., 0, :]
    kim = sf[..., 1, :] - sb[..., 1, :]
    knyq = sf[..., 1, :] + sb[..., 1, :]
    kbin = (jnp.arange(nf)[:, None] * tf + jnp.arange(tf)[None, :])
    is0 = (kbin == 0)[:, None, None, :]
    wgt = jnp.where(is0, 1.0, 2.0) / (2.0 * l)
    ca = kre * wgt
    cb = jnp.where(is0, 0.0, -kim) * wgt
    cc = jnp.where(is0, 0.0, kim) * wgt
    cd = jnp.where(is0, knyq, kre) * wgt
    coef = jnp.stack([ca, cb, cc, cd], axis=2)
    return coef.transpose(1, 0, 2, 3, 4)


def _mlqk_kernel(u_ref, w_ref, o_ref):
    u = u_ref[0].astype(F32)
    l = u.shape[0]
    t = lax.broadcasted_iota(jnp.int32, u.shape, 0)
    prev = jnp.where(t == 0, 0.0, pltpu.roll(u, 1, 0))
    nxt = jnp.where(t == l - 1, 0.0, pltpu.roll(u, l - 1, 0))
    y = _silu(prev * w_ref[0:1, :] + u * w_ref[1:2, :] + nxt * w_ref[2:3, :])
    lane = lax.broadcasted_iota(jnp.int32, u.shape, 1)
    o_ref[0] = jnp.where(lane >= ML_W, y * (ML_DH ** -0.5), y).astype(BF16)


def _mlqk(u_main, ml_conv):
    b, l, _ = u_main.shape
    w = 2 * ML_W
    return pl.pallas_call(
        _mlqk_kernel,
        grid=(b,),
        in_specs=[pl.BlockSpec((1, l, w), lambda i: (i, 0, MLM_OFF // w)),
                  pl.BlockSpec((3, w), lambda i: (0, 0))],
        out_specs=pl.BlockSpec((1, l, w), lambda i: (i, 0, 0)),
        out_shape=jax.ShapeDtypeStruct((b, l, w), BF16),
        compiler_params=_cp(("parallel",), VMEM_LIMIT),
        name="mlqk",
    )(u_main, ml_conv)


def _head_norm_pair(xb, lo):
    def halves(v):
        s_all = jnp.sum(v, axis=1, keepdims=True)
        s_lo = jnp.sum(jnp.where(lo, v, 0.0), axis=1, keepdims=True)
        return jnp.where(lo, s_lo, s_all - s_lo) * (1.0 / 64.0)
    xc = xb - halves(xb)
    return xc * lax.rsqrt(halves(xc * xc) + EPS)


def _scan_kernel(*refs, reverse, use_rope, c):
    it = iter(refs)
    lg_ref = next(it)
    ret_ref = next(it)
    if use_rope:
        cos_ref, sa_ref, sb_ref = next(it), next(it), next(it)
    mlqk_ref = next(it)
    mlvo_ref = next(it)
    mlg_ref = next(it)
    gb_ref = next(it)
    rs0_ref, mc0_ref, mn0_ref, mm0_ref = next(it), next(it), next(it), next(it)
    if reverse:
        of_ref, hf_ref = next(it), next(it)
    o_ref, h_ref = next(it), next(it)
    rs_out, mc_out, mn_out, mm_out = next(it), next(it), next(it), next(it)
    dec_scr, qd_scr, kd_scr, tri_scr, rs_scr, mc_scr, mn_scr, mm_scr = (
        next(it), next(it), next(it), next(it), next(it), next(it), next(it), next(it))

    ci = pl.program_id(1)
    nc = pl.num_programs(1)
    d = 1 if reverse else 0

    tt = lax.broadcasted_iota(jnp.int32, (c, c), 0)
    ss = lax.broadcasted_iota(jnp.int32, (c, c), 1)
    rel = (ss - tt) if reverse else (tt - ss)
    valid = rel >= 0
    lane = lax.broadcasted_iota(jnp.int32, (c, LANES), 1)
    lo = lane < 64
    trow = lax.broadcasted_iota(jnp.int32, (c, LANES), 0)
    r128 = lax.broadcasted_iota(jnp.int32, (LANES, LANES), 0)
    c128 = lax.broadcasted_iota(jnp.int32, (LANES, LANES), 1)
    bdiag = (r128 < 64) == (c128 < 64)
    lo_row = lax.broadcasted_iota(jnp.int32, (1, LANES), 1) < 64

    @pl.when(ci == 0)
    def _():
        relf = jnp.where(valid, rel, 0).astype(F32)
        qpow = ((c - trow) if reverse else (trow + 1)).astype(F32)
        kpow = (trow if reverse else (c - 1 - trow)).astype(F32)
        for h in range(RET_HEADS):
            dec_scr[h] = jnp.where(valid, jnp.exp(relf * lg_ref[h]), 0.0)
        for p in range(RET_HEADS // 2):
            lg_l = jnp.where(lo, lg_ref[2 * p], lg_ref[2 * p + 1])
            qd_scr[p] = jnp.exp(qpow * lg_l)
            kd_scr[p] = jnp.exp(kpow * lg_l)
        tri_scr[...] = jnp.where(valid, 1.0, 0.0).astype(BF16)
        rs_scr[...] = rs0_ref[0]
        mc_scr[...] = mc0_ref[0]
        mn_scr[...] = mn0_ref[0]
        mm_scr[...] = mm0_ref[0]

    u = ret_ref[0]
    for p in range(RET_HEADS // 2):
        sl = slice(p * LANES, (p + 1) * LANES)
        q = u[:, p * LANES:(p + 1) * LANES].astype(F32) * (RET_DH ** -0.5)
        k = u[:, RET_W + p * LANES:RET_W + (p + 1) * LANES].astype(F32)
        v = u[:, 2 * RET_W + p * LANES:2 * RET_W + (p + 1) * LANES]
        if use_rope:
            cs, sa, sb = cos_ref[:, sl], sa_ref[:, sl], sb_ref[:, sl]
            q = q * cs + pltpu.roll(q, LANES - 16, 1) * sa + pltpu.roll(q, 16, 1) * sb
            k = k * cs + pltpu.roll(k, LANES - 16, 1) * sa + pltpu.roll(k, 16, 1) * sb
        qb = q.astype(BF16)
        kb = k.astype(BF16)
        outs = []
        for hh in range(2):
            qm = jnp.where(lo == (hh == 0), qb, jnp.zeros_like(qb))
            s = _dot_nt(qm, kb) * dec_scr[2 * p + hh]
            outs.append(_dot(s.astype(BF16), v))
        o_intra = jnp.where(lo, outs[0], outs[1])
        st = rs_scr[p]
        o_inter = _dot(qb, st.astype(BF16)) * qd_scr[p]
        o_new = o_intra + o_inter
        kw_t = (k * kd_scr[p]).T.astype(BF16)
        upd = _dot(kw_t, v)
        cdec = jnp.exp(jnp.where(lo_row, lg_ref[2 * p], lg_ref[2 * p + 1]) * float(c))
        rs_scr[p] = st * cdec + jnp.where(bdiag, upd, 0.0)
        if reverse:
            o_tot = of_ref[0, :, sl] + o_new
            g = u[:, 3 * RET_W + p * LANES:3 * RET_W + (p + 1) * LANES].astype(F32)
            o_ref[0, :, sl] = _head_norm_pair(o_tot, lo) * _silu(g)
        else:
            o_ref[0, :, sl] = o_new

    gates = mlg_ref[0] + gb_ref[...]
    lf = _log_sigmoid(gates)
    tri = tri_scr[...]
    l_hi, l_mid, l_lo = _split3(lf)
    cum = _dot(tri, l_hi) + _dot(tri, l_mid) + _dot(tri, l_lo)
    cum_t = cum.T
    gates_t = gates.T
    last = 0 if reverse else c - 1
    qk = mlqk_ref[0]
    vo = mlvo_ref[0]
    for p in range(ML_HEADS // 2):
        sl = slice(p * LANES, (p + 1) * LANES)
        qb = qk[:, p * LANES:(p + 1) * LANES]
        kb = qk[:, ML_W + p * LANES:ML_W + (p + 1) * LANES]
        v = vo[:, p * LANES:(p + 1) * LANES]
        cst = mc_scr[p]
        nrow = mn_scr[p]
        inter_num = _dot(qb, cst.astype(BF16))
        qf = qb.astype(F32)
        kf = kb.astype(F32)
        qn = qf * nrow
        nums, dens, a_l, m_l, wk_l, as_l = [], [], [], [], [], []
        for hh in range(2):
            h = 2 * p + hh
            ic = d * 8 + h
            fc = d * 8 + 4 + h
            b_col = cum[:, fc:fc + 1]
            b_row = cum_t[fc:fc + 1, :]
            i_row = gates_t[ic:ic + 1, :]
            i_col = gates[:, ic:ic + 1]
            m_prev = mm_scr[h:h + 1, 0:1]
            dlog = jnp.where(valid, b_col - b_row + i_row, -jnp.inf)
            inter = b_col + m_prev
            m_t = jnp.maximum(inter, jnp.max(dlog, axis=1, keepdims=True))
            w = jnp.exp(dlog - m_t)
            a = jnp.exp(inter - m_t)
            qm = jnp.where(lo == (hh == 0), qb, jnp.zeros_like(qb))
            s = _dot_nt(qm, kb) * w
            nums.append(_dot(s.astype(BF16), v))
            qn_h = jnp.sum(jnp.where(lo == (hh == 0), qn, 0.0), axis=1, keepdims=True)
            dens.append(jnp.sum(s, axis=1, keepdims=True) + a * qn_h)
            a_l.append(a)
            m_l.append(m_t)
            total = cum[last:last + 1, fc:fc + 1]
            wlog = total - b_col + i_col
            m_new = jnp.maximum(total + m_prev, jnp.max(wlog, axis=0, keepdims=True))
            wk_l.append(jnp.exp(wlog - m_new))
            as_l.append(jnp.exp(total + m_prev - m_new))
            mm_scr[h:h + 1, :] = jnp.broadcast_to(m_new, (1, LANES))
        a_lanes = jnp.where(lo, a_l[0], a_l[1])
        num = jnp.where(lo, nums[0], nums[1]) + a_lanes * inter_num
        den = jnp.where(lo, dens[0], dens[1])
        m_lanes = jnp.where(lo, m_l[0], m_l[1])
        h_new = num / jnp.maximum(jnp.abs(den), jnp.exp(-m_lanes))
        kw = kf * jnp.where(lo, wk_l[0], wk_l[1])
        as_row = jnp.where(lo_row, as_l[0], as_l[1])
        upd = _dot(kw.T.astype(BF16), v)
        mc_scr[p] = cst * as_row + jnp.where(bdiag, upd, 0.0)
        mn_scr[p] = nrow * as_row + jnp.sum(kw, axis=0, keepdims=True)
        if reverse:
            h_tot = hf_ref[0, :, sl] + h_new
            og = vo[:, ML_W + p * LANES:ML_W + (p + 1) * LANES].astype(F32)
            h_ref[0, :, sl] = _sigmoid(og) * _head_norm_pair(h_tot, lo)
        else:
            h_ref[0, :, sl] = h_new

    @pl.when(ci == nc - 1)
    def _():
        rs_out[0] = rs_scr[...]
        mc_out[0] = mc_scr[...]
        mn_out[0] = mn_scr[...]
        mm_out[0] = mm_scr[...]


def _scan(u_main, u_gates, mlqk, rope, lg, gate_bias, states, prev, reverse):
    b, l, _ = u_main.shape
    c = min(CHUNK, l)
    nc = l // c
    use_rope = rope is not None
    cix = (lambda i: nc - 1 - i) if reverse else (lambda i: i)
    in_specs = [pl.BlockSpec(memory_space=pltpu.SMEM),
                pl.BlockSpec((1, c, 4 * RET_W), lambda bi, i: (bi, cix(i), RETM_OFF // (4 * RET_W)))]
    args = [lg, u_main]
    if use_rope:
        in_specs += [pl.BlockSpec((c, RET_W), lambda bi, i: (cix(i), 0))] * 3
        args += list(rope)
    in_specs += [pl.BlockSpec((1, c, 2 * ML_W), lambda bi, i: (bi, cix(i), 0)),
                 pl.BlockSpec((1, c, 2 * ML_W), lambda bi, i: (bi, cix(i), MLM_OFF // (2 * ML_W) + 1)),
                 pl.BlockSpec((1, c, LANES), lambda bi, i: (bi, cix(i), 0)),
                 pl.BlockSpec((1, LANES), lambda bi, i: (0, 0)),
                 pl.BlockSpec((1, 3, LANES, LANES), lambda bi, i: (bi, 0, 0, 0)),
                 pl.BlockSpec((1, 2, LANES, LANES), lambda bi, i: (bi, 0, 0, 0)),
                 pl.BlockSpec((1, 2, 1, LANES), lambda bi, i: (bi, 0, 0, 0)),
                 pl.BlockSpec((1, 8, LANES), lambda bi, i: (bi, 0, 0))]
    args += [mlqk, u_main, u_gates, gate_bias, *states]
    if reverse:
        in_specs += [pl.BlockSpec((1, c, RET_W), lambda bi, i: (bi, cix(i), 0)),
                     pl.BlockSpec((1, c, ML_W), lambda bi, i: (bi, cix(i), 0))]
        args += list(prev)
    out_specs = [pl.BlockSpec((1, c, RET_W), lambda bi, i: (bi, cix(i), 0)),
                 pl.BlockSpec((1, c, ML_W), lambda bi, i: (bi, cix(i), 0)),
                 pl.BlockSpec((1, 3, LANES, LANES), lambda bi, i: (bi, 0, 0, 0)),
                 pl.BlockSpec((1, 2, LANES, LANES), lambda bi, i: (bi, 0, 0, 0)),
                 pl.BlockSpec((1, 2, 1, LANES), lambda bi, i: (bi, 0, 0, 0)),
                 pl.BlockSpec((1, 8, LANES), lambda bi, i: (bi, 0, 0))]
    out_shape = [jax.ShapeDtypeStruct((b, l, RET_W), F32),
                 jax.ShapeDtypeStruct((b, l, ML_W), F32),
                 jax.ShapeDtypeStruct((b, 3, LANES, LANES), F32),
                 jax.ShapeDtypeStruct((b, 2, LANES, LANES), F32),
                 jax.ShapeDtypeStruct((b, 2, 1, LANES), F32),
                 jax.ShapeDtypeStruct((b, 8, LANES), F32)]
    scratch = [pltpu.VMEM((RET_HEADS, c, c), F32),
               pltpu.VMEM((3, c, LANES), F32),
               pltpu.VMEM((3, c, LANES), F32),
               pltpu.VMEM((c, c), BF16),
               pltpu.VMEM((3, LANES, LANES), F32),
               pltpu.VMEM((2, LANES, LANES), F32),
               pltpu.VMEM((2, 1, LANES), F32),
               pltpu.VMEM((8, LANES), F32)]
    kern = functools.partial(_scan_kernel, reverse=reverse, use_rope=use_rope, c=c)
    outs = pl.pallas_call(
        kern, grid=(b, nc), in_specs=in_specs, out_specs=out_specs, out_shape=out_shape,
        scratch_shapes=scratch,
        compiler_params=_cp(("parallel", "arbitrary"), VMEM_LIMIT),
        name="scan_bwd" if reverse else "scan_fwd",
    )(*args)
    return (outs[0], outs[1]), tuple(outs[2:])


def _merge_kernel(yh_ref, yr_ref, ym_ref, gt_ref, x_ref, mg_ref, wa_ref, wb_ref, wc_ref, wo_ref, o_ref):
    d = x_ref.shape[2]
    g = gt_ref[0].astype(F32)
    yh = yh_ref[0].T.astype(BF16)
    acc = _sigmoid(g[:, 0:d]) * _dot(yh, wa_ref[...])
    acc += _sigmoid(g[:, d:2 * d]) * _dot(yr_ref[0].astype(BF16), wb_ref[...])
    acc += _sigmoid(g[:, 2 * d:3 * d]) * _dot(ym_ref[0].astype(BF16), wc_ref[...])
    y = _dot(acc.astype(BF16), wo_ref[...])
    o_ref[0] = x_ref[0] + mg_ref[0] * y


def _merge(y_hyt, y_ret, y_ml, u_main, x, mgate, w_up, w_out, tm):
    b, l, d = x.shape
    per_batch_mod = mgate.shape[0] > 1
    mi = (lambda bi, i: (bi, 0, 0)) if per_batch_mod else (lambda bi, i: (0, 0, 0))
    wa, wb, wc = w_up[:HY_W], w_up[HY_W:HY_W + RET_W], w_up[HY_W + RET_W:]
    const = lambda bi, i: (0, 0)
    return pl.pallas_call(
        _merge_kernel,
        grid=(b, l // tm),
        in_specs=[pl.BlockSpec((1, HY_W, tm), lambda bi, i: (bi, 0, i)),
                  pl.BlockSpec((1, tm, RET_W), lambda bi, i: (bi, i, 0)),
                  pl.BlockSpec((1, tm, ML_W), lambda bi, i: (bi, i, 0)),
                  pl.BlockSpec((1, tm, 3 * d), lambda bi, i: (bi, i, 0)),
                  pl.BlockSpec((1, tm, d), lambda bi, i: (bi, i, 0)),
                  pl.BlockSpec((1, 1, d), mi),
                  pl.BlockSpec(wa.shape, const),
                  pl.BlockSpec(wb.shape, const),
                  pl.BlockSpec(wc.shape, const),
                  pl.BlockSpec(w_out.shape, const)],
        out_specs=pl.BlockSpec((1, tm, d), lambda bi, i: (bi, i, 0)),
        out_shape=jax.ShapeDtypeStruct((b, l, d), F32),
        compiler_params=_cp(("parallel", "parallel"), VMEM_LIMIT),
        name="merge",
    )(y_hyt, y_ret, y_ml, u_main, x, mgate, wa, wb, wc, w_out)


def _ffn_kernel(*refs, final):
    it = iter(refs)
    x_ref, g_ref, sh_ref, sc_ref, mg_ref = next(it), next(it), next(it), next(it), next(it)
    w1_ref, w3_ref, w2_ref = next(it), next(it), next(it)
    if final:
        gf_ref = next(it)
    o_ref = next(it)
    h_scr, acc_scr = next(it), next(it)
    f = pl.program_id(2)

    @pl.when(f == 0)
    def _():
        h = _norm_mod(x_ref[0], g_ref[...], sh_ref[0], sc_ref[0])
        h_scr[...] = h.astype(BF16)
        acc_scr[...] = jnp.zeros_like(acc_scr)

    h = h_scr[...]
    act = _silu(_dot(h, w1_ref[...])) * _dot(h, w3_ref[...])
    acc_scr[...] += _dot(act.astype(BF16), w2_ref[...])

    @pl.when(f == pl.num_programs(2) - 1)
    def _():
        y = x_ref[0] + mg_ref[0] * acc_scr[...]
        if final:
            y = y * lax.rsqrt(jnp.mean(y * y, axis=-1, keepdims=True) + EPS) * gf_ref[...]
        o_ref[0] = y


def _ffn(x, g, shift, scale, mgate, w1, w3, w2, g_final, tm, tf):
    b, l, d = x.shape
    ff = w1.shape[1]
    final = g_final is not None
    per_batch_mod = shift.shape[0] > 1
    mi = (lambda bi, i, f: (bi, 0, 0)) if per_batch_mod else (lambda bi, i, f: (0, 0, 0))
    const2 = lambda bi, i, f: (0, 0)
    in_specs = [pl.BlockSpec((1, tm, d), lambda bi, i, f: (bi, i, 0)),
                pl.BlockSpec((1, d), const2),
                pl.BlockSpec((1, 1, d), mi), pl.BlockSpec((1, 1, d), mi), pl.BlockSpec((1, 1, d), mi),
                pl.BlockSpec((d, tf), lambda bi, i, f: (0, f)),
                pl.BlockSpec((d, tf), lambda bi, i, f: (0, f)),
                pl.BlockSpec((tf, d), lambda bi, i, f: (f, 0))]
    args = [x, g, shift, scale, mgate, w1, w3, w2]
    if final:
        in_specs += [pl.BlockSpec((1, d), const2)]
        args += [g_final]
    return pl.pallas_call(
        functools.partial(_ffn_kernel, final=final),
        grid=(b, l // tm, ff // tf),
        in_specs=in_specs,
        out_specs=pl.BlockSpec((1, tm, d), lambda bi, i, f: (bi, i, 0)),
        out_shape=jax.ShapeDtypeStruct((b, l, d), F32),
        scratch_shapes=[pltpu.VMEM((tm, d), BF16), pltpu.VMEM((tm, d), F32)],
        compiler_params=_cp(("parallel", "parallel", "arbitrary"), VMEM_LIMIT),
        name="ffn",
    )(*args)


def _top2(logits):
    lane = lax.broadcasted_iota(jnp.int32, logits.shape, 1)
    neg = jnp.float32(-jnp.inf)
    logits = jnp.where(lane < N_EXPERTS, logits, neg)
    m1 = jnp.max(logits, axis=1, keepdims=True)
    i1 = jnp.min(jnp.where(logits == m1, lane, LANES), axis=1, keepdims=True)
    rest = jnp.where(lane == i1, neg, logits)
    m2 = jnp.max(rest, axis=1, keepdims=True)
    i2 = jnp.min(jnp.where(rest == m2, lane, LANES), axis=1, keepdims=True)
    e2 = jnp.exp(m2 - m1)
    return lane, i1, i2, 1.0 / (1.0 + e2), e2 / (1.0 + e2)


def _route_kernel(x_ref, g_ref, sh_ref, sc_ref, wr_ref, br_ref, h_ref, meta_ref, cnt_ref, cnt_scr):
    first = jnp.logical_and(pl.program_id(0) == 0, pl.program_id(1) == 0)

    @pl.when(first)
    def _():
        cnt_scr[...] = jnp.zeros_like(cnt_scr)

    h = _norm_mod(x_ref[0], g_ref[...], sh_ref[0], sc_ref[0])
    h_ref[0] = h
    tm = h.shape[0]
    lane, i1, i2, w1, w2 = _top2(_dot3(h, wr_ref[...]) + br_ref[...])
    oh1 = (lane == i1).astype(F32)
    oh2 = (lane == i2).astype(F32)
    both = oh1 + oh2
    tt = lax.broadcasted_iota(jnp.int32, (tm, tm), 0)
    ss = lax.broadcasted_iota(jnp.int32, (tm, tm), 1)
    before = jnp.where(tt > ss, 1.0, 0.0).astype(BF16)
    pref = _dot(before, both.astype(BF16)) + cnt_scr[...]
    r1 = jnp.sum(pref * oh1, axis=1, keepdims=True)
    r2 = jnp.sum(pref * oh2, axis=1, keepdims=True)
    cnt_scr[...] += jnp.sum(both, axis=0, keepdims=True)
    cnt_ref[...] = cnt_scr[...]
    vals = (i1.astype(F32), i2.astype(F32), r1, r2, w1, w2)
    meta = jnp.zeros((tm, LANES), F32)
    for j, val in enumerate(vals):
        meta = jnp.where(lane == j, val, meta)
    meta_ref[0] = meta[:, :8]


def _route(x, g, shift, scale, wr, br, tm):
    b, l, d = x.shape
    const2 = lambda bi, i: (0, 0)
    return pl.pallas_call(
        _route_kernel,
        grid=(b, l // tm),
        in_specs=[pl.BlockSpec((1, tm, d), lambda bi, i: (bi, i, 0)),
                  pl.BlockSpec((1, d), const2),
                  pl.BlockSpec((1, 1, d), lambda bi, i: (bi, 0, 0)),
                  pl.BlockSpec((1, 1, d), lambda bi, i: (bi, 0, 0)),
                  pl.BlockSpec((d, LANES), const2),
                  pl.BlockSpec((1, LANES), const2)],
        out_specs=[pl.BlockSpec((1, tm, d), lambda bi, i: (bi, i, 0)),
                   pl.BlockSpec((1, tm, 8), lambda bi, i: (bi, i, 0)),
                   pl.BlockSpec((1, LANES), const2)],
        out_shape=[jax.ShapeDtypeStruct((b, l, d), F32),
                   jax.ShapeDtypeStruct((b, l, 8), F32),
                   jax.ShapeDtypeStruct((1, LANES), F32)],
        scratch_shapes=[pltpu.VMEM((1, LANES), F32)],
        compiler_params=_cp(("arbitrary", "arbitrary"), VMEM_LIMIT),
        name="route",
    )(x, g, shift, scale, wr, br)


def _row_copy(src_ref, src_row, dst_ref, dst_row, sem):
    return pltpu.make_async_copy(src_ref.at[pl.ds(src_row, 1)], dst_ref.at[pl.ds(dst_row, 1)], sem)


ROW_UNROLL = 8


def _scatter_kernel(dest_ref, h_ref, init_ref, xs_ref, sem):
    del init_ref
    tm = h_ref.shape[0]

    def issue(j, carry):
        for k in range(ROW_UNROLL):
            r = j * ROW_UNROLL + k
            _row_copy(h_ref, r, xs_ref, dest_ref[0, 0, r], sem).start(priority=0)
            _row_copy(h_ref, r, xs_ref, dest_ref[0, 0, tm + r], sem).start(priority=1)
        return carry

    lax.fori_loop(0, tm // ROW_UNROLL, issue, 0)
    for _ in range(2):
        pltpu.make_async_copy(h_ref, xs_ref.at[pl.ds(0, tm)], sem).wait()


def _scatter_rows(h_flat, dest, n_rows, tm):
    t, d = h_flat.shape
    nt = t // tm
    dest_blk = jnp.concatenate([dest[0].reshape(nt, 1, tm), dest[1].reshape(nt, 1, tm)], axis=2)
    return pl.pallas_call(
        _scatter_kernel,
        grid=(nt,),
        in_specs=[pl.BlockSpec((1, 1, 2 * tm), lambda i: (i, 0, 0), memory_space=pltpu.SMEM),
                  pl.BlockSpec((tm, d), lambda i: (i, 0)),
                  pl.BlockSpec(memory_space=pl.ANY)],
        out_specs=pl.BlockSpec(memory_space=pl.ANY),
        out_shape=jax.ShapeDtypeStruct((n_rows, d), F32),
        scratch_shapes=[pltpu.SemaphoreType.DMA],
        input_output_aliases={2: 0},
        compiler_params=_cp(("arbitrary",), VMEM_LIMIT),
        name="moe_scatter",
    )(dest_blk, h_flat, jnp.zeros((n_rows, d), F32))


def _experts_kernel(te_ref, nv_ref, x_ref, w1_ref, w3_ref, w2_ref, o_ref, h_scr, acc_scr):
    del te_ref
    i = pl.program_id(0)
    f = pl.program_id(1)
    live = i < nv_ref[0]

    @pl.when(jnp.logical_and(live, f == 0))
    def _():
        h_scr[...] = x_ref[...].astype(BF16)
        acc_scr[...] = jnp.zeros_like(acc_scr)

    @pl.when(live)
    def _():
        h = h_scr[...]
        act = _silu(_dot(h, w1_ref[0])) * _dot(h, w3_ref[0])
        acc_scr[...] += _dot(act.astype(BF16), w2_ref[0])

    @pl.when(f == pl.num_programs(1) - 1)
    def _():
        o_ref[...] = jnp.where(live, acc_scr[...], 0.0)


def _experts(xs, tile_expert, n_valid, w1, w3, w2, tm, tf):
    p, d = xs.shape
    ff = w1.shape[2]
    nt = p // tm
    row = lambda i, f, te, nv: (jnp.minimum(i, nv[0] - 1), 0)
    grid_spec = pltpu.PrefetchScalarGridSpec(
        num_scalar_prefetch=2,
        grid=(nt, ff // tf),
        in_specs=[pl.BlockSpec((tm, d), row),
                  pl.BlockSpec((1, d, tf), lambda i, f, te, nv: (te[i], 0, f)),
                  pl.BlockSpec((1, d, tf), lambda i, f, te, nv: (te[i], 0, f)),
                  pl.BlockSpec((1, tf, d), lambda i, f, te, nv: (te[i], f, 0))],
        out_specs=pl.BlockSpec((tm, d), lambda i, f, te, nv: (i, 0)),
        scratch_shapes=[pltpu.VMEM((tm, d), BF16), pltpu.VMEM((tm, d), F32)])
    return pl.pallas_call(
        _experts_kernel,
        grid_spec=grid_spec,
        out_shape=jax.ShapeDtypeStruct((p, d), F32),
        compiler_params=_cp(("arbitrary", "arbitrary"), VMEM_LIMIT),
        name="moe_experts",
    )(tile_expert, n_valid, xs, w1, w3, w2)


def _combine_kernel(dest_ref, meta_ref, x_ref, mg_ref, gf_ref, ys_ref, o_ref, ybuf, sem, *, final):
    tm = x_ref.shape[1]

    def issue(j, carry):
        for k in range(ROW_UNROLL):
            r = j * ROW_UNROLL + k
            _row_copy(ys_ref, dest_ref[0, 0, r], ybuf.at[0], r, sem).start(priority=0)
            _row_copy(ys_ref, dest_ref[0, 0, tm + r], ybuf.at[1], r, sem).start(priority=1)
        return carry

    lax.fori_loop(0, tm // ROW_UNROLL, issue, 0)
    for slot in range(2):
        pltpu.make_async_copy(ys_ref.at[pl.ds(0, tm)], ybuf.at[slot], sem).wait()
    meta = meta_ref[0]
    y = x_ref[0] + mg_ref[0] * (meta[:, 4:5] * ybuf[0] + meta[:, 5:6] * ybuf[1])
    if final:
        y = y * lax.rsqrt(jnp.mean(y * y, axis=-1, keepdims=True) + EPS) * gf_ref[...]
    o_ref[0] = y


def _combine(ys, dest, meta, x, mgate, g_final, tm):
    b, l, d = x.shape
    nt = l // tm
    final = g_final is not None
    gf = g_final if final else jnp.ones((1, d), F32)
    dest_blk = jnp.concatenate([dest[0].reshape(b * nt, 1, tm), dest[1].reshape(b * nt, 1, tm)], axis=2)
    return pl.pallas_call(
        functools.partial(_combine_kernel, final=final),
        grid=(b, nt),
        in_specs=[pl.BlockSpec((1, 1, 2 * tm), lambda bi, i: (bi * nt + i, 0, 0), memory_space=pltpu.SMEM),
                  pl.BlockSpec((1, tm, 8), lambda bi, i: (bi, i, 0)),
                  pl.BlockSpec((1, tm, d), lambda bi, i: (bi, i, 0)),
                  pl.BlockSpec((1, 1, d), lambda bi, i: (bi, 0, 0)),
                  pl.BlockSpec((1, d), lambda bi, i: (0, 0)),
                  pl.BlockSpec(memory_space=pl.ANY)],
        out_specs=pl.BlockSpec((1, tm, d), lambda bi, i: (bi, i, 0)),
        out_shape=jax.ShapeDtypeStruct((b, l, d), F32),
        scratch_shapes=[pltpu.VMEM((2, tm, d), F32), pltpu.SemaphoreType.DMA],
        compiler_params=_cp(("arbitrary", "arbitrary"), VMEM_LIMIT),
        name="moe_combine",
    )(dest_blk, meta, x, mgate, gf, ys)


def _moe_ffn(x, g, shift, scale, mgate, wr, br, w1, w3, w2, g_final, tm_e, tf):
    b, l, d = x.shape
    t = b * l
    ne = w1.shape[0]
    h, meta, cnt = _route(x, g, shift, scale, wr, br, min(512, l))
    e_idx = jnp.stack([meta[..., 0], meta[..., 1]]).astype(jnp.int32).reshape(2, t)
    rank = jnp.stack([meta[..., 2], meta[..., 3]]).astype(jnp.int32).reshape(2, t)
    counts = cnt[0, :ne].astype(jnp.int32)
    padded = ((counts + tm_e - 1) // tm_e) * tm_e
    ends = jnp.cumsum(padded)
    first_row = ends - padded
    dest = rank
    for e in range(ne):
        dest = dest + jnp.where(e_idx == e, first_row[e], 0)
    n_tiles = (2 * t) // tm_e + ne
    starts = jnp.arange(n_tiles, dtype=jnp.int32) * tm_e
    tile_expert = jnp.minimum(jnp.sum(starts[:, None] >= ends[None, :], axis=1), ne - 1).astype(jnp.int32)
    n_valid = (ends[-1] // tm_e).astype(jnp.int32).reshape(1)
    xs = _scatter_rows(h.reshape(t, d), dest, n_tiles * tm_e, min(512, l))
    ys = _experts(xs, tile_expert, n_valid, w1, w3, w2, tm_e, tf)
    return _combine(ys, dest, meta, x, mgate, g_final, min(512, l))


def _rope_tables(n_lat):
    rows = n_lat // GRID_W
    row_id = jnp.broadcast_to(jnp.arange(rows)[:, None], (rows, GRID_W)).reshape(-1).astype(F32)
    col_id = jnp.broadcast_to(jnp.arange(GRID_W)[None, :], (rows, GRID_W)).reshape(-1).astype(F32)
    q4 = RET_DH // 4
    inv = 1.0 / (ROPE_BASE ** (jnp.arange(q4, dtype=F32) / q4))
    ang_r = row_id[:, None] * inv[None, :]
    ang_c = col_id[:, None] * inv[None, :]
    zeros = jnp.zeros_like(ang_r)
    cos_h = jnp.concatenate([jnp.cos(ang_r)] * 2 + [jnp.cos(ang_c)] * 2, axis=-1)
    sin_a = jnp.concatenate([-jnp.sin(ang_r), zeros, -jnp.sin(ang_c), zeros], axis=-1)
    sin_b = jnp.concatenate([zeros, jnp.sin(ang_r), zeros, jnp.sin(ang_c)], axis=-1)
    tile = lambda t: jnp.tile(t, (1, RET_HEADS))
    return tile(cos_h), tile(sin_a), tile(sin_b)


def _zero_states(b):
    return (jnp.zeros((b, 3, LANES, LANES), F32), jnp.zeros((b, 2, LANES, LANES), F32),
            jnp.zeros((b, 2, 1, LANES), F32), jnp.zeros((b, 8, LANES), F32))


def kernel(x, c, ctx, c_ctx, w_mod, b_mod, g_mix, g_ffn, w_in, b_in, hy_conv, hy_f1, hy_fb1, hy_f2,
           hy_fb2, hy_f3, hy_decay, hy_skip, ret_decay, ml_conv, ml_gate_bias, w_up, w_out, ffn_w1,
           ffn_w3, ffn_w2, moe_router, moe_router_b, moe_w1, moe_w3, moe_w2, g_final):
    b, l, d = x.shape
    lc = ctx.shape[1]
    depth = w_mod.shape[0]
    hy_off, ret_off = 0, 3 * HY_W
    ml_off = ret_off + 4 * RET_W
    mlg_off = ml_off + 4 * ML_W
    gate_off = mlg_off + 4 * ML_HEADS

    mp = ((b + 1 + 7) // 8) * 8
    c_all = jnp.zeros((mp, d), F32).at[:b].set(c).at[b].set(c_ctx)
    mods = _mod(c_all, w_mod, b_mod).reshape(depth, mp, 6, d)

    rope = _rope_tables(l)
    tf_l, tf_c = min(256, l), min(256, lc)
    fwd_l = _dft_matrix(l, tf_l)
    fwd_c = _dft_matrix(lc, tf_c)
    tm_l = min(1024, l)
    tm_c = min(1024, b * lc)

    xl, xc = x, ctx
    for layer in range(depth):
        last = layer == depth - 1
        ml_ = mods[layer, :b][:, :, None, :]
        mc_ = mods[layer, b][None, :, None, :]
        wi = w_in[layer]
        bi = b_in[layer]
        w_main = jnp.concatenate(
            [wi[:, gate_off:], wi[:, ret_off:ml_off], wi[:, ml_off:mlg_off]], axis=1).astype(BF16)
        b_main = jnp.concatenate([bi[gate_off:], bi[ret_off:ml_off], bi[ml_off:mlg_off]])[None, :]
        w_hyt = wi[:, hy_off:ret_off].T.astype(BF16)
        b_hy = bi[hy_off:ret_off][:, None]
        w_g = jnp.zeros((d, LANES), F32).at[:, :4 * ML_HEADS].set(wi[:, mlg_off:gate_off]).astype(BF16)
        b_g = jnp.zeros((1, LANES), F32).at[0, :4 * ML_HEADS].set(bi[mlg_off:gate_off])
        gmix = g_mix[layer][None, :]

        um_l, uh_l, ug_l = _in_proj(xl, gmix, ml_[:, 0], ml_[:, 1], w_main, b_main, w_hyt, b_hy,
                                    w_g, b_g, tm_l)
        xc_flat = xc.reshape(1, b * lc, d)
        um_c, uh_c, ug_c = _in_proj(xc_flat, gmix, mc_[:, 0], mc_[:, 1], w_main, b_main, w_hyt, b_hy,
                                    w_g, b_g, tm_c)
        um_c = um_c.reshape(b, lc, N_MAIN)
        ug_c = ug_c.reshape(b, lc, LANES)

        lg = jax.nn.log_sigmoid(ret_decay[layer].astype(F32))
        gbias = jnp.zeros((1, LANES), F32).at[0, :4 * ML_HEADS].set(ml_gate_bias[layer].reshape(-1))
        qk_c = _mlqk(um_c, ml_conv[layer])
        qk_l = _mlqk(um_l, ml_conv[layer])
        z0 = _zero_states(b)
        pf_c, sf = _scan(um_c, ug_c, qk_c, None, lg[0], gbias, z0, None, False)
        yc_rm, sb = _scan(um_c, ug_c, qk_c, None, lg[1], gbias, z0, pf_c, True)
        pf_l, _ = _scan(um_l, ug_l, qk_l, rope, lg[0], gbias, sf, None, False)
        yl_rm, _ = _scan(um_l, ug_l, qk_l, rope, lg[1], gbias, sb, pf_l, True)

        w_conv_t = hy_conv[layer].T
        skip = hy_skip[layer][:, :, None]

        def hyena(uh, length, fwd, bt):
            filt = _hyena_filters(length, hy_f1[layer], hy_fb1[layer], hy_f2[layer], hy_fb2[layer],
                                  hy_f3[layer], hy_decay[layer])
            coef = _filter_coefs(filt, fwd)
            z1 = _hyena_conv(uh, 0, uh, 1, w_conv_t, skip[0], fwd, coef[0], True, bt)
            return _hyena_conv(z1, 0, uh, 2, w_conv_t, skip[1], fwd, coef[1], False, bt)

        yh_l = hyena(uh_l, l, fwd_l, 1)
        wup = w_up[layer].astype(BF16)
        wout = w_out[layer].astype(BF16)
        xl = _merge(yh_l, yl_rm[0], yl_rm[1], um_l, xl, ml_[:, 2], wup, wout, min(512, l))
        if not last:
            uh_cb = uh_c.reshape(3 * HY_W, b, lc).transpose(1, 0, 2)
            yh_c = hyena(uh_cb, lc, fwd_c, b)
            xc = _merge(yh_c, yc_rm[0], yc_rm[1], um_c, xc, mc_[:, 2], wup, wout, min(512, lc))

        j = layer // 2
        gffn = g_ffn[layer][None, :]
        gfin = g_final[None, :] if last else None
        xcf = xc.reshape(1, b * lc, d)
        if layer % 2 == 0:
            w1 = ffn_w1[j].astype(BF16)
            w3 = ffn_w3[j].astype(BF16)
            w2 = ffn_w2[j].astype(BF16)
            tf = _ff_tile(w1.shape[1])
            xl = _ffn(xl, gffn, ml_[:, 3], ml_[:, 4], ml_[:, 5], w1, w3, w2, gfin, min(512, l), tf)
            if not last:
                xcf = _ffn(xcf, gffn, mc_[:, 3], mc_[:, 4], mc_[:, 5], w1, w3, w2, None, min(512, b * lc), tf)
        else:
            w1 = moe_w1[j].astype(BF16)
            w3 = moe_w3[j].astype(BF16)
            w2 = moe_w2[j].astype(BF16)
            wr = jnp.zeros((d, LANES), F32).at[:, :N_EXPERTS].set(moe_router[j])
            br = jnp.zeros((1, LANES), F32).at[0, :N_EXPERTS].set(moe_router_b[j])
            tf = _ff_tile(w1.shape[2])
            xl = _moe_ffn(xl, gffn, ml_[:, 3], ml_[:, 4], ml_[:, 5], wr, br, w1, w3, w2, gfin, MOE_ROWS, tf)
            if not last:
                xcf = _moe_ffn(xcf, gffn, mc_[:, 3], mc_[:, 4], mc_[:, 5], wr, br, w1, w3, w2, None, MOE_ROWS, tf)
        xc = xcf.reshape(b, lc, d)
    if depth == 0:
        raise ValueError("depth must be positive")
    return xl
```

```python
import functools
import math

import jax
import jax.numpy as jnp
from jax import lax
from jax.experimental import pallas as pl
from jax.experimental.pallas import tpu as pltpu

F32 = jnp.float32
BF16 = jnp.bfloat16
EPS = 1e-6

GRID_W = 64
HY_W = 384
RET_HEADS, RET_DH = 6, 64
RET_W = RET_HEADS * RET_DH
ML_HEADS, ML_DH = 4, 64
ML_W = ML_HEADS * ML_DH
HY_EMB = 33
ROPE_BASE = 10000.0
N_EXPERTS = 8
LANES = 128

GATE_OFF = 0
RETM_OFF = 3 * 1024
MLM_OFF = RETM_OFF + 4 * RET_W
N_MAIN = MLM_OFF + 4 * ML_W
N_TILE = 1408
CHUNK = 256

MOE_ROWS = 1024

VMEM_LIMIT = 56 * 1024 * 1024


def _ff_tile(ff):
    return ff // 2 if (ff // 2) % LANES == 0 else ff


def _cp(sem, vmem=None):
    return pltpu.CompilerParams(dimension_semantics=sem, vmem_limit_bytes=vmem)


def _split3(a):
    hi = a.astype(BF16)
    r1 = a - hi.astype(F32)
    mid = r1.astype(BF16)
    lo = (r1 - mid.astype(F32)).astype(BF16)
    return hi, mid, lo


def _dot(a, b):
    return jnp.dot(a, b, preferred_element_type=F32)


def _dot_nt(a, b):
    return lax.dot_general(a, b, (((1,), (1,)), ((), ())), preferred_element_type=F32)


def _dot3(a, b):
    ah, am, al = _split3(a)
    bh, bm, bl = _split3(b)
    return (_dot(ah, bh) + (_dot(ah, bm) + _dot(am, bh))
            + (_dot(am, bm) + _dot(ah, bl) + _dot(al, bh)))


def _sigmoid(x):
    return 1.0 / (1.0 + jnp.exp(-x))


def _silu(x):
    return x * _sigmoid(x)


def _log_sigmoid(x):
    return jnp.minimum(x, 0.0) - jnp.log(1.0 + jnp.exp(-jnp.abs(x)))


def _mod_kernel(c_ref, w_ref, b_ref, o_ref):
    s = _silu(c_ref[...])
    o_ref[0] = _dot3(s, w_ref[0]) + b_ref[0]


def _mod(c_all, w_mod, b_mod):
    depth, d, n = w_mod.shape
    mp = c_all.shape[0]
    tn = 1536
    return pl.pallas_call(
        _mod_kernel,
        grid=(depth, n // tn),
        in_specs=[pl.BlockSpec((mp, d), lambda l, j: (0, 0)),
                  pl.BlockSpec((1, d, tn), lambda l, j: (l, 0, j)),
                  pl.BlockSpec((1, 1, tn), lambda l, j: (l, 0, j))],
        out_specs=pl.BlockSpec((1, mp, tn), lambda l, j: (l, 0, j)),
        out_shape=jax.ShapeDtypeStruct((depth, mp, n), F32),
        compiler_params=_cp(("parallel", "parallel"), VMEM_LIMIT),
        name="mod",
    )(c_all, w_mod, b_mod.reshape(depth, 1, n))


def _norm_mod(x, g, shift, scale):
    y = x * lax.rsqrt(jnp.mean(x * x, axis=-1, keepdims=True) + EPS)
    return (y * g) * (1.0 + scale) + shift


def _inproj_kernel(x_ref, g_ref, sh_ref, sc_ref, wm_ref, bm_ref, wh_ref, bh_ref, wg_ref, bg_ref,
                   um_ref, uh_ref, ug_ref, h_scr):
    n = pl.program_id(2)

    @pl.when(n == 0)
    def _():
        h = _norm_mod(x_ref[0], g_ref[...], sh_ref[0], sc_ref[0]).astype(BF16)
        h_scr[...] = h
        uh_ref[0] = _dot_nt(wh_ref[...], h) + bh_ref[...]
        ug_ref[0] = _dot(h, wg_ref[...]) + bg_ref[...]

    um_ref[0] = (_dot(h_scr[...], wm_ref[...]) + bm_ref[...]).astype(BF16)


def _in_proj(x, g, shift, scale, w_main, b_main, w_hyt, b_hy, w_g, b_g, tm):
    bx, lx, d = x.shape
    per_batch_mod = shift.shape[0] > 1
    mi = (lambda b, i, n: (b, 0, 0)) if per_batch_mod else (lambda b, i, n: (0, 0, 0))
    nh = w_hyt.shape[0]
    return pl.pallas_call(
        _inproj_kernel,
        grid=(bx, lx // tm, N_MAIN // N_TILE),
        in_specs=[pl.BlockSpec((1, tm, d), lambda b, i, n: (b, i, 0)),
                  pl.BlockSpec((1, d), lambda b, i, n: (0, 0)),
                  pl.BlockSpec((1, 1, d), mi),
                  pl.BlockSpec((1, 1, d), mi),
                  pl.BlockSpec((d, N_TILE), lambda b, i, n: (0, n)),
                  pl.BlockSpec((1, N_TILE), lambda b, i, n: (0, n)),
                  pl.BlockSpec((nh, d), lambda b, i, n: (0, 0)),
                  pl.BlockSpec((nh, 1), lambda b, i, n: (0, 0)),
                  pl.BlockSpec((d, LANES), lambda b, i, n: (0, 0)),
                  pl.BlockSpec((1, LANES), lambda b, i, n: (0, 0))],
        out_specs=[pl.BlockSpec((1, tm, N_TILE), lambda b, i, n: (b, i, n)),
                   pl.BlockSpec((1, nh, tm), lambda b, i, n: (b, 0, i)),
                   pl.BlockSpec((1, tm, LANES), lambda b, i, n: (b, i, 0))],
        out_shape=[jax.ShapeDtypeStruct((bx, lx, N_MAIN), BF16),
                   jax.ShapeDtypeStruct((bx, nh, lx), F32),
                   jax.ShapeDtypeStruct((bx, lx, LANES), F32)],
        scratch_shapes=[pltpu.VMEM((tm, d), BF16)],
        compiler_params=_cp(("parallel", "parallel", "arbitrary"), VMEM_LIMIT),
        name="in_proj",
    )(x, g, shift, scale, w_main, b_main, w_hyt, b_hy, w_g, b_g)


def _dft_kernel(h_ref, f_ref, o_ref):
    hh, hm, hl = _split3(h_ref[...])
    f = f_ref[0]
    o_ref[0] = _dot(hh, f) + _dot(hm, f) + _dot(hl, f)


def _dft(h_rows, fwd):
    r, l = h_rows.shape
    nf, _, tw = fwd.shape
    return pl.pallas_call(
        _dft_kernel,
        grid=(nf,),
        in_specs=[pl.BlockSpec((r, l), lambda f: (0, 0)),
                  pl.BlockSpec((1, l, tw), lambda f: (f, 0, 0))],
        out_specs=pl.BlockSpec((1, r, tw), lambda f: (f, 0, 0)),
        out_shape=jax.ShapeDtypeStruct((nf, r, tw), F32),
        compiler_params=_cp(("parallel",), VMEM_LIMIT),
        name="dft",
    )(h_rows, fwd)


def _short_conv_lanes(u, w):
    l = u.shape[1]
    t = lax.broadcasted_iota(jnp.int32, u.shape, 1)
    prev = jnp.where(t == 0, 0.0, pltpu.roll(u, 1, 1))
    nxt = jnp.where(t == l - 1, 0.0, pltpu.roll(u, l - 1, 1))
    return prev * w[:, 0:1] + u * w[:, 1:2] + nxt * w[:, 2:3]


def _hyena_kernel(in_ref, gate_ref, wi_ref, wg_ref, skip_ref, f_ref, coef_ref, o_ref,
                  inb_scr, y_scr, *rest, conv_in, bt, cw, tf):
    f = pl.program_id(1)
    nf = pl.num_programs(1)

    @pl.when(f == 0)
    def _():
        for i in range(bt):
            u = in_ref[i]
            if conv_in:
                u = _short_conv_lanes(u, wi_ref[...])
                rest[0][i * cw:(i + 1) * cw, :] = u
            inb_scr[i * cw:(i + 1) * cw, :] = u.astype(BF16)
        o_ref[...] = jnp.zeros_like(o_ref)

    fw = f_ref[0]
    spec = _dot(inb_scr[...], fw)
    ca, cb, cc, cd = coef_ref[0, 0], coef_ref[0, 1], coef_ref[0, 2], coef_ref[0, 3]
    for i in range(bt):
        ur = spec[i * cw:(i + 1) * cw, :tf]
        ui = spec[i * cw:(i + 1) * cw, tf:]
        y_scr[i * cw:(i + 1) * cw, :tf] = (ur * ca + ui * cb).astype(BF16)
        y_scr[i * cw:(i + 1) * cw, tf:] = (ur * cc + ui * cd).astype(BF16)
    back = _dot_nt(y_scr[...], fw)
    for i in range(bt):
        o_ref[i] += back[i * cw:(i + 1) * cw, :]

    @pl.when(f == nf - 1)
    def _():
        for i in range(bt):
            gate = _short_conv_lanes(gate_ref[i], wg_ref[...])
            u = rest[0][i * cw:(i + 1) * cw, :] if conv_in else in_ref[i]
            o_ref[i] = gate * (o_ref[i] + skip_ref[...] * u)


def _hyena_conv(src, src_blk, gate_src, gate_blk, w_conv_t, skip, fwd, coef, conv_in, bt):
    b, _, l = src.shape
    cw = HY_W
    nf, _, tw = fwd.shape
    tf = tw // 2
    kern = functools.partial(_hyena_kernel, conv_in=conv_in, bt=bt, cw=cw, tf=tf)
    scratch = [pltpu.VMEM((bt * cw, l), BF16), pltpu.VMEM((bt * cw, tw), BF16)]
    if conv_in:
        scratch += [pltpu.VMEM((bt * cw, l), F32)]
    return pl.pallas_call(
        kern,
        grid=(b // bt, nf),
        in_specs=[pl.BlockSpec((bt, cw, l), lambda i, f: (i, src_blk, 0)),
                  pl.BlockSpec((bt, cw, l), lambda i, f: (i, gate_blk, 0)),
                  pl.BlockSpec((cw, 3), lambda i, f: (src_blk if conv_in else 0, 0)),
                  pl.BlockSpec((cw, 3), lambda i, f: (gate_blk, 0)),
                  pl.BlockSpec((cw, 1), lambda i, f: (0, 0)),
                  pl.BlockSpec((1, l, tw), lambda i, f: (f, 0, 0)),
                  pl.BlockSpec((1, 4, cw, tf), lambda i, f: (f, 0, 0, 0))],
        out_specs=pl.BlockSpec((bt, cw, l), lambda i, f: (i, 0, 0)),
        out_shape=jax.ShapeDtypeStruct((b, cw, l), F32),
        scratch_shapes=scratch,
        compiler_params=_cp(("parallel", "arbitrary"), VMEM_LIMIT),
        name="hyena_conv",
    )(src, gate_src, w_conv_t, w_conv_t, skip, fwd, coef)


def _dft_matrix(l, tf):
    n = jnp.arange(l, dtype=jnp.int32)[:, None]
    k = jnp.arange(l, dtype=jnp.int32)[None, :]
    blk = 64 if l % 64 == 0 else 1
    hi = (jnp.arange(l // blk, dtype=jnp.int32)[:, None] * blk * k) % (2 * l)
    lo = (jnp.arange(blk, dtype=jnp.int32)[:, None] * k) % (2 * l)
    ang_hi = hi.astype(F32)[:, None, :] * (math.pi / l)
    ang_lo = lo.astype(F32)[None, :, :] * (math.pi / l)
    ch, sh, cl, sl = jnp.cos(ang_hi), jnp.sin(ang_hi), jnp.cos(ang_lo), jnp.sin(ang_lo)
    re = (ch * cl - sh * sl).reshape(l, l)
    im = -(sh * cl + ch * sl).reshape(l, l)
    nyq = jnp.where(n % 2 == 0, 1.0, -1.0).astype(F32)
    im = jnp.where(k == 0, nyq, im)
    nf = l // tf
    re = re.reshape(l, nf, tf)
    im = im.reshape(l, nf, tf)
    return jnp.concatenate([re, im], axis=-1).transpose(1, 0, 2).astype(BF16)


def _hyena_filters(length, f1, fb1, f2, fb2, f3, decay):
    hp = lax.Precision.HIGHEST
    pos = jnp.arange(length, dtype=F32)
    t = pos / max(length - 1, 1)
    n_bands = (HY_EMB - 1) // 2
    bands = jnp.linspace(1e-4, n_bands - 1, n_bands, dtype=F32)
    z = (2.0 * math.pi * pos / length)[:, None] * bands[None, :]
    feats = jnp.concatenate([t[:, None], jnp.cos(z), -jnp.sin(z)], axis=-1)
    hid = jnp.sin(jnp.dot(feats, f1, precision=hp) + fb1)
    hid = jnp.sin(jnp.dot(hid, f2, precision=hp) + fb2)
    h = jnp.dot(hid, f3, precision=hp)
    window = jnp.exp(-t[:, None] * jnp.abs(decay)[None, :])
    return h * window


def _filter_coefs(filt, fwd):
    l = filt.shape[0]
    nf, _, tw = fwd.shape
    tf = tw // 2
    rows = filt.T
    is_bwd = (jnp.arange(rows.shape[0]) // HY_W) % 2 == 1
    first = jnp.arange(l)[None, :] == 0
    rows = jnp.where(is_bwd[:, None] & first, 0.0, rows)
    spec = _dft(rows, fwd)
    spec = spec.reshape(nf, 2, 2, HY_W, 2, tf)
    sf, sb = spec[:, :, 0], spec[:, :, 1]
    kre = sf[..., 0, :] + sb[..., 0, :]
    kim = sf[..., 1, :] - sb[..., 1, :]
    knyq = sf[..., 1, :] + sb[..., 1, :]
    kbin = (jnp.arange(nf)[:, None] * tf + jnp.arange(tf)[None, :])
    is0 = (kbin == 0)[:, None, None, :]
    wgt = jnp.where(is0, 1.0, 2.0) / (2.0 * l)
    ca = kre * wgt
    cb = jnp.where(is0, 0.0, -kim) * wgt
    cc = jnp.where(is0, 0.0, kim) * wgt
    cd = jnp.where(is0, knyq, kre) * wgt
    coef = jnp.stack([ca, cb, cc, cd], axis=2)
    return coef.transpose(1, 0, 2, 3, 4)


def _mlqk_kernel(u_ref, w_ref, o_ref):
    u = u_ref[0].astype(F32)
    l = u.shape[0]
    t = lax.broadcasted_iota(jnp.int32, u.shape, 0)
    prev = jnp.where(t == 0, 0.0, pltpu.roll(u, 1, 0))
    nxt = jnp.where(t == l - 1, 0.0, pltpu.roll(u, l - 1, 0))
    y = _silu(prev * w_ref[0:1, :] + u * w_ref[1:2, :] + nxt * w_ref[2:3, :])
    lane = lax.broadcasted_iota(jnp.int32, u.shape, 1)
    o_ref[0] = jnp.where(lane >= ML_W, y * (ML_DH ** -0.5), y).astype(BF16)


def _mlqk(u_main, ml_conv):
    b, l, _ = u_main.shape
    w = 2 * ML_W
    return pl.pallas_call(
        _mlqk_kernel,
        grid=(b,),
        in_specs=[pl.BlockSpec((1, l, w), lambda i: (i, 0, MLM_OFF // w)),
                  pl.BlockSpec((3, w), lambda i: (0, 0))],
        out_specs=pl.BlockSpec((1, l, w), lambda i: (i, 0, 0)),
        out_shape=jax.ShapeDtypeStruct((b, l, w), BF16),
        compiler_params=_cp(("parallel",), VMEM_LIMIT),
        name="mlqk",
    )(u_main, ml_conv)


def _head_norm_pair(xb, lo):
    def halves(v):
        s_all = jnp.sum(v, axis=1, keepdims=True)
        s_lo = jnp.sum(jnp.where(lo, v, 0.0), axis=1, keepdims=True)
        return jnp.where(lo, s_lo, s_all - s_lo) * (1.0 / 64.0)
    xc = xb - halves(xb)
    return xc * lax.rsqrt(halves(xc * xc) + EPS)


def _scan_kernel(*refs, reverse, use_rope, c):
    it = iter(refs)
    lg_ref = next(it)
    ret_ref = next(it)
    if use_rope:
        cos_ref, sa_ref, sb_ref = next(it), next(it), next(it)
    mlqk_ref = next(it)
    mlvo_ref = next(it)
    mlg_ref = next(it)
    gb_ref = next(it)
    rs0_ref, mc0_ref, mn0_ref, mm0_ref = next(it), next(it), next(it), next(it)
    if reverse:
        of_ref, hf_ref = next(it), next(it)
    o_ref, h_ref = next(it), next(it)
    rs_out, mc_out, mn_out, mm_out = next(it), next(it), next(it), next(it)
    dec_scr, qd_scr, kd_scr, tri_scr, rs_scr, mc_scr, mn_scr, mm_scr = (
        next(it), next(it), next(it), next(it), next(it), next(it), next(it), next(it))

    ci = pl.program_id(1)
    nc = pl.num_programs(1)
    d = 1 if reverse else 0

    tt = lax.broadcasted_iota(jnp.int32, (c, c), 0)
    ss = lax.broadcasted_iota(jnp.int32, (c, c), 1)
    rel = (ss - tt) if reverse else (tt - ss)
    valid = rel >= 0
    lane = lax.broadcasted_iota(jnp.int32, (c, LANES), 1)
    lo = lane < 64
    trow = lax.broadcasted_iota(jnp.int32, (c, LANES), 0)
    r128 = lax.broadcasted_iota(jnp.int32, (LANES, LANES), 0)
    c128 = lax.broadcasted_iota(jnp.int32, (LANES, LANES), 1)
    bdiag = (r128 < 64) == (c128 < 64)
    lo_row = lax.broadcasted_iota(jnp.int32, (1, LANES), 1) < 64

    @pl.when(ci == 0)
    def _():
        relf = jnp.where(valid, rel, 0).astype(F32)
        qpow = ((c - trow) if reverse else (trow + 1)).astype(F32)
        kpow = (trow if reverse else (c - 1 - trow)).astype(F32)
        for h in range(RET_HEADS):
            dec_scr[h] = jnp.where(valid, jnp.exp(relf * lg_ref[h]), 0.0)
        for p in range(RET_HEADS // 2):
            lg_l = jnp.where(lo, lg_ref[2 * p], lg_ref[2 * p + 1])
            qd_scr[p] = jnp.exp(qpow * lg_l)
            kd_scr[p] = jnp.exp(kpow * lg_l)
        tri_scr[...] = jnp.where(valid, 1.0, 0.0).astype(BF16)
        rs_scr[...] = rs0_ref[0]
        mc_scr[...] = mc0_ref[0]
        mn_scr[...] = mn0_ref[0]
        mm_scr[...] = mm0_ref[0]

    u = ret_ref[0]
    for p in range(RET_HEADS // 2):
        sl = slice(p * LANES, (p + 1) * LANES)
        q = u[:, p * LANES:(p + 1) * LANES].astype(F32) * (RET_DH ** -0.5)
        k = u[:, RET_W + p * LANES:RET_W + (p + 1) * LANES].astype(F32)
        v = u[:, 2 * RET_W + p * LANES:2 * RET_W + (p + 1) * LANES]
        if use_rope:
            cs, sa, sb = cos_ref[:, sl], sa_ref[:, sl], sb_ref[:, sl]
            q = q * cs + pltpu.roll(q, LANES - 16, 1) * sa + pltpu.roll(q, 16, 1) * sb
            k = k * cs + pltpu.roll(k, LANES - 16, 1) * sa + pltpu.roll(k, 16, 1) * sb
        qb = q.astype(BF16)
        kb = k.astype(BF16)
        outs = []
        for hh in range(2):
            qm = jnp.where(lo == (hh == 0), qb, jnp.zeros_like(qb))
            s = _dot_nt(qm, kb) * dec_scr[2 * p + hh]
            outs.append(_dot(s.astype(BF16), v))
        o_intra = jnp.where(lo, outs[0], outs[1])
        st = rs_scr[p]
        o_inter = _dot(qb, st.astype(BF16)) * qd_scr[p]
        o_new = o_intra + o_inter
        kw_t = (k * kd_scr[p]).T.astype(BF16)
        upd = _dot(kw_t, v)
        cdec = jnp.exp(jnp.where(lo_row, lg_ref[2 * p], lg_ref[2 * p + 1]) * float(c))
        rs_scr[p] = st * cdec + jnp.where(bdiag, upd, 0.0)
        if reverse:
            o_tot = of_ref[0, :, sl] + o_new
            g = u[:, 3 * RET_W + p * LANES:3 * RET_W + (p + 1) * LANES].astype(F32)
            o_ref[0, :, sl] = _head_norm_pair(o_tot, lo) * _silu(g)
        else:
            o_ref[0, :, sl] = o_new

    gates = mlg_ref[0] + gb_ref[...]
    lf = _log_sigmoid(gates)
    tri = tri_scr[...]
    l_hi, l_mid, l_lo = _split3(lf)
    cum = _dot(tri, l_hi) + _dot(tri, l_mid) + _dot(tri, l_lo)
    cum_t = cum.T
    gates_t = gates.T
    last = 0 if reverse else c - 1
    qk = mlqk_ref[0]
    vo = mlvo_ref[0]
    for p in range(ML_HEADS // 2):
        sl = slice(p * LANES, (p + 1) * LANES)
        qb = qk[:, p * LANES:(p + 1) * LANES]
        kb = qk[:, ML_W + p * LANES:ML_W + (p + 1) * LANES]
        v = vo[:, p * LANES:(p + 1) * LANES]
        cst = mc_scr[p]
        nrow = mn_scr[p]
        inter_num = _dot(qb, cst.astype(BF16))
        qf = qb.astype(F32)
        kf = kb.astype(F32)
        qn = qf * nrow
        nums, dens, a_l, m_l, wk_l, as_l = [], [], [], [], [], []
        for hh in range(2):
            h = 2 * p + hh
            ic = d * 8 + h
            fc = d * 8 + 4 + h
            b_col = cum[:, fc:fc + 1]
            b_row = cum_t[fc:fc + 1, :]
            i_row = gates_t[ic:ic + 1, :]
            i_col = gates[:, ic:ic + 1]
            m_prev = mm_scr[h:h + 1, 0:1]
            dlog = jnp.where(valid, b_col - b_row + i_row, -jnp.inf)
            inter = b_col + m_prev
            m_t = jnp.maximum(inter, jnp.max(dlog, axis=1, keepdims=True))
            w = jnp.exp(dlog - m_t)
            a = jnp.exp(inter - m_t)
            qm = jnp.where(lo == (hh == 0), qb, jnp.zeros_like(qb))
            s = _dot_nt(qm, kb) * w
            nums.append(_dot(s.astype(BF16), v))
            qn_h = jnp.sum(jnp.where(lo == (hh == 0), qn, 0.0), axis=1, keepdims=True)
            dens.append(jnp.sum(s, axis=1, keepdims=True) + a * qn_h)
            a_l.append(a)
            m_l.append(m_t)
            total = cum[last:last + 1, fc:fc + 1]
            wlog = total - b_col + i_col
            m_new = jnp.maximum(total + m_prev, jnp.max(wlog, axis=0, keepdims=True))
            wk_l.append(jnp.exp(wlog - m_new))
            as_l.append(jnp.exp(total + m_prev - m_new))
            mm_scr[h:h + 1, :] = jnp.broadcast_to(m_new, (1, LANES))
        a_lanes = jnp.where(lo, a_l[0], a_l[1])
        num = jnp.where(lo, nums[0], nums[1]) + a_lanes * inter_num
        den = jnp.where(lo, dens[0], dens[1])
        m_lanes = jnp.where(lo, m_l[0], m_l[1])
        h_new = num / jnp.maximum(jnp.abs(den), jnp.exp(-m_lanes))
        kw = kf * jnp.where(lo, wk_l[0], wk_l[1])
        as_row = jnp.where(lo_row, as_l[0], as_l[1])
        upd = _dot(kw.T.astype(BF16), v)
        mc_scr[p] = cst * as_row + jnp.where(bdiag, upd, 0.0)
        mn_scr[p] = nrow * as_row + jnp.sum(kw, axis=0, keepdims=True)
        if reverse:
            h_tot = hf_ref[0, :, sl] + h_new
            og = vo[:, ML_W + p * LANES:ML_W + (p + 1) * LANES].astype(F32)
            h_ref[0, :, sl] = _sigmoid(og) * _head_norm_pair(h_tot, lo)
        else:
            h_ref[0, :, sl] = h_new

    @pl.when(ci == nc - 1)
    def _():
        rs_out[0] = rs_scr[...]
        mc_out[0] = mc_scr[...]
        mn_out[0] = mn_scr[...]
        mm_out[0] = mm_scr[...]


def _scan(u_main, u_gates, mlqk, rope, lg, gate_bias, states, prev, reverse):
    b, l, _ = u_main.shape
    c = min(CHUNK, l)
    nc = l // c
    use_rope = rope is not None
    cix = (lambda i: nc - 1 - i) if reverse else (lambda i: i)
    in_specs = [pl.BlockSpec(memory_space=pltpu.SMEM),
                pl.BlockSpec((1, c, 4 * RET_W), lambda bi, i: (bi, cix(i), RETM_OFF // (4 * RET_W)))]
    args = [lg, u_main]
    if use_rope:
        in_specs += [pl.BlockSpec((c, RET_W), lambda bi, i: (cix(i), 0))] * 3
        args += list(rope)
    in_specs += [pl.BlockSpec((1, c, 2 * ML_W), lambda bi, i: (bi, cix(i), 0)),
                 pl.BlockSpec((1, c, 2 * ML_W), lambda bi, i: (bi, cix(i), MLM_OFF // (2 * ML_W) + 1)),
                 pl.BlockSpec((1, c, LANES), lambda bi, i: (bi, cix(i), 0)),
                 pl.BlockSpec((1, LANES), lambda bi, i: (0, 0)),
                 pl.BlockSpec((1, 3, LANES, LANES), lambda bi, i: (bi, 0, 0, 0)),
                 pl.BlockSpec((1, 2, LANES, LANES), lambda bi, i: (bi, 0, 0, 0)),
                 pl.BlockSpec((1, 2, 1, LANES), lambda bi, i: (bi, 0, 0, 0)),
                 pl.BlockSpec((1, 8, LANES), lambda bi, i: (bi, 0, 0))]
    args += [mlqk, u_main, u_gates, gate_bias, *states]
    if reverse:
        in_specs += [pl.BlockSpec((1, c, RET_W), lambda bi, i: (bi, cix(i), 0)),
                     pl.BlockSpec((1, c, ML_W), lambda bi, i: (bi, cix(i), 0))]
        args += list(prev)
    out_specs = [pl.BlockSpec((1, c, RET_W), lambda bi, i: (bi, cix(i), 0)),
                 pl.BlockSpec((1, c, ML_W), lambda bi, i: (bi, cix(i), 0)),
                 pl.BlockSpec((1, 3, LANES, LANES), lambda bi, i: (bi, 0, 0, 0)),
                 pl.BlockSpec((1, 2, LANES, LANES), lambda bi, i: (bi, 0, 0, 0)),
                 pl.BlockSpec((1, 2, 1, LANES), lambda bi, i: (bi, 0, 0, 0)),
                 pl.BlockSpec((1, 8, LANES), lambda bi, i: (bi, 0, 0))]
    out_shape = [jax.ShapeDtypeStruct((b, l, RET_W), F32),
                 jax.ShapeDtypeStruct((b, l, ML_W), F32),
                 jax.ShapeDtypeStruct((b, 3, LANES, LANES), F32),
                 jax.ShapeDtypeStruct((b, 2, LANES, LANES), F32),
                 jax.ShapeDtypeStruct((b, 2, 1, LANES), F32),
                 jax.ShapeDtypeStruct((b, 8, LANES), F32)]
    scratch = [pltpu.VMEM((RET_HEADS, c, c), F32),
               pltpu.VMEM((3, c, LANES), F32),
               pltpu.VMEM((3, c, LANES), F32),
               pltpu.VMEM((c, c), BF16),
               pltpu.VMEM((3, LANES, LANES), F32),
               pltpu.VMEM((2, LANES, LANES), F32),
               pltpu.VMEM((2, 1, LANES), F32),
               pltpu.VMEM((8, LANES), F32)]
    kern = functools.partial(_scan_kernel, reverse=reverse, use_rope=use_rope, c=c)
    outs = pl.pallas_call(
        kern, grid=(b, nc), in_specs=in_specs, out_specs=out_specs, out_shape=out_shape,
        scratch_shapes=scratch,
        compiler_params=_cp(("parallel", "arbitrary"), VMEM_LIMIT),
        name="scan_bwd" if reverse else "scan_fwd",
    )(*args)
    return (outs[0], outs[1]), tuple(outs[2:])


def _merge_kernel(yh_ref, yr_ref, ym_ref, gt_ref, x_ref, mg_ref, wa_ref, wb_ref, wc_ref, wo_ref, o_ref):
    d = x_ref.shape[2]
    g = gt_ref[0].astype(F32)
    yh = yh_ref[0].T.astype(BF16)
    acc = _sigmoid(g[:, 0:d]) * _dot(yh, wa_ref[...])
    acc += _sigmoid(g[:, d:2 * d]) * _dot(yr_ref[0].astype(BF16), wb_ref[...])
    acc += _sigmoid(g[:, 2 * d:3 * d]) * _dot(ym_ref[0].astype(BF16), wc_ref[...])
    y = _dot(acc.astype(BF16), wo_ref[...])
    o_ref[0] = x_ref[0] + mg_ref[0] * y


def _merge(y_hyt, y_ret, y_ml, u_main, x, mgate, w_up, w_out, tm):
    b, l, d = x.shape
    per_batch_mod = mgate.shape[0] > 1
    mi = (lambda bi, i: (bi, 0, 0)) if per_batch_mod else (lambda bi, i: (0, 0, 0))
    wa, wb, wc = w_up[:HY_W], w_up[HY_W:HY_W + RET_W], w_up[HY_W + RET_W:]
    const = lambda bi, i: (0, 0)
    return pl.pallas_call(
        _merge_kernel,
        grid=(b, l // tm),
        in_specs=[pl.BlockSpec((1, HY_W, tm), lambda bi, i: (bi, 0, i)),
                  pl.BlockSpec((1, tm, RET_W), lambda bi, i: (bi, i, 0)),
                  pl.BlockSpec((1, tm, ML_W), lambda bi, i: (bi, i, 0)),
                  pl.BlockSpec((1, tm, 3 * d), lambda bi, i: (bi, i, 0)),
                  pl.BlockSpec((1, tm, d), lambda bi, i: (bi, i, 0)),
                  pl.BlockSpec((1, 1, d), mi),
                  pl.BlockSpec(wa.shape, const),
                  pl.BlockSpec(wb.shape, const),
                  pl.BlockSpec(wc.shape, const),
                  pl.BlockSpec(w_out.shape, const)],
        out_specs=pl.BlockSpec((1, tm, d), lambda bi, i: (bi, i, 0)),
        out_shape=jax.ShapeDtypeStruct((b, l, d), F32),
        compiler_params=_cp(("parallel", "parallel"), VMEM_LIMIT),
        name="merge",
    )(y_hyt, y_ret, y_ml, u_main, x, mgate, wa, wb, wc, w_out)


def _ffn_kernel(*refs, final):
    it = iter(refs)
    x_ref, g_ref, sh_ref, sc_ref, mg_ref = next(it), next(it), next(it), next(it), next(it)
    w1_ref, w3_ref, w2_ref = next(it), next(it), next(it)
    if final:
        gf_ref = next(it)
    o_ref = next(it)
    h_scr, acc_scr = next(it), next(it)
    f = pl.program_id(2)

    @pl.when(f == 0)
    def _():
        h = _norm_mod(x_ref[0], g_ref[...], sh_ref[0], sc_ref[0])
        h_scr[...] = h.astype(BF16)
        acc_scr[...] = jnp.zeros_like(acc_scr)

    h = h_scr[...]
    act = _silu(_dot(h, w1_ref[...])) * _dot(h, w3_ref[...])
    acc_scr[...] += _dot(act.astype(BF16), w2_ref[...])

    @pl.when(f == pl.num_programs(2) - 1)
    def _():
        y = x_ref[0] + mg_ref[0] * acc_scr[...]
        if final:
            y = y * lax.rsqrt(jnp.mean(y * y, axis=-1, keepdims=True) + EPS) * gf_ref[...]
        o_ref[0] = y


def _ffn(x, g, shift, scale, mgate, w1, w3, w2, g_final, tm, tf):
    b, l, d = x.shape
    ff = w1.shape[1]
    final = g_final is not None
    per_batch_mod = shift.shape[0] > 1
    mi = (lambda bi, i, f: (bi, 0, 0)) if per_batch_mod else (lambda bi, i, f: (0, 0, 0))
    const2 = lambda bi, i, f: (0, 0)
    in_specs = [pl.BlockSpec((1, tm, d), lambda bi, i, f: (bi, i, 0)),
                pl.BlockSpec((1, d), const2),
                pl.BlockSpec((1, 1, d), mi), pl.BlockSpec((1, 1, d), mi), pl.BlockSpec((1, 1, d), mi),
                pl.BlockSpec((d, tf), lambda bi, i, f: (0, f)),
                pl.BlockSpec((d, tf), lambda bi, i, f: (0, f)),
                pl.BlockSpec((tf, d), lambda bi, i, f: (f, 0))]
    args = [x, g, shift, scale, mgate, w1, w3, w2]
    if final:
        in_specs += [pl.BlockSpec((1, d), const2)]
        args += [g_final]
    return pl.pallas_call(
        functools.partial(_ffn_kernel, final=final),
        grid=(b, l // tm, ff // tf),
        in_specs=in_specs,
        out_specs=pl.BlockSpec((1, tm, d), lambda bi, i, f: (bi, i, 0)),
        out_shape=jax.ShapeDtypeStruct((b, l, d), F32),
        scratch_shapes=[pltpu.VMEM((tm, d), BF16), pltpu.VMEM((tm, d), F32)],
        compiler_params=_cp(("parallel", "parallel", "arbitrary"), VMEM_LIMIT),
        name="ffn",
    )(*args)


def _top2(logits):
    lane = lax.broadcasted_iota(jnp.int32, logits.shape, 1)
    neg = jnp.float32(-jnp.inf)
    logits = jnp.where(lane < N_EXPERTS, logits, neg)
    m1 = jnp.max(logits, axis=1, keepdims=True)
    i1 = jnp.min(jnp.where(logits == m1, lane, LANES), axis=1, keepdims=True)
    rest = jnp.where(lane == i1, neg, logits)
    m2 = jnp.max(rest, axis=1, keepdims=True)
    i2 = jnp.min(jnp.where(rest == m2, lane, LANES), axis=1, keepdims=True)
    e2 = jnp.exp(m2 - m1)
    return lane, i1, i2, 1.0 / (1.0 + e2), e2 / (1.0 + e2)


def _route_kernel(x_ref, g_ref, sh_ref, sc_ref, wr_ref, br_ref, h_ref, meta_ref, cnt_ref, cnt_scr):
    first = jnp.logical_and(pl.program_id(0) == 0, pl.program_id(1) == 0)

    @pl.when(first)
    def _():
        cnt_scr[...] = jnp.zeros_like(cnt_scr)

    h = _norm_mod(x_ref[0], g_ref[...], sh_ref[0], sc_ref[0])
    h_ref[0] = h
    tm = h.shape[0]
    lane, i1, i2, w1, w2 = _top2(_dot3(h, wr_ref[...]) + br_ref[...])
    oh1 = (lane == i1).astype(F32)
    oh2 = (lane == i2).astype(F32)
    both = oh1 + oh2
    tt = lax.broadcasted_iota(jnp.int32, (tm, tm), 0)
    ss = lax.broadcasted_iota(jnp.int32, (tm, tm), 1)
    before = jnp.where(tt > ss, 1.0, 0.0).astype(BF16)
    pref = _dot(before, both.astype(BF16)) + cnt_scr[...]
    r1 = jnp.sum(pref * oh1, axis=1, keepdims=True)
    r2 = jnp.sum(pref * oh2, axis=1, keepdims=True)
    cnt_scr[...] += jnp.sum(both, axis=0, keepdims=True)
    cnt_ref[...] = cnt_scr[...]
    vals = (i1.astype(F32), i2.astype(F32), r1, r2, w1, w2)
    meta = jnp.zeros((tm, LANES), F32)
    for j, val in enumerate(vals):
        meta = jnp.where(lane == j, val, meta)
    meta_ref[0] = meta[:, :8]


def _route(x, g, shift, scale, wr, br, tm):
    b, l, d = x.shape
    const2 = lambda bi, i: (0, 0)
    return pl.pallas_call(
        _route_kernel,
        grid=(b, l // tm),
        in_specs=[pl.BlockSpec((1, tm, d), lambda bi, i: (bi, i, 0)),
                  pl.BlockSpec((1, d), const2),
                  pl.BlockSpec((1, 1, d), lambda bi, i: (bi, 0, 0)),
                  pl.BlockSpec((1, 1, d), lambda bi, i: (bi, 0, 0)),
                  pl.BlockSpec((d, LANES), const2),
                  pl.BlockSpec((1, LANES), const2)],
        out_specs=[pl.BlockSpec((1, tm, d), lambda bi, i: (bi, i, 0)),
                   pl.BlockSpec((1, tm, 8), lambda bi, i: (bi, i, 0)),
                   pl.BlockSpec((1, LANES), const2)],
        out_shape=[jax.ShapeDtypeStruct((b, l, d), F32),
                   jax.ShapeDtypeStruct((b, l, 8), F32),
                   jax.ShapeDtypeStruct((1, LANES), F32)],
        scratch_shapes=[pltpu.VMEM((1, LANES), F32)],
        compiler_params=_cp(("arbitrary", "arbitrary"), VMEM_LIMIT),
        name="route",
    )(x, g, shift, scale, wr, br)


def _row_copy(src_ref, src_row, dst_ref, dst_row, sem):
    return pltpu.make_async_copy(src_ref.at[pl.ds(src_row, 1)], dst_ref.at[pl.ds(dst_row, 1)], sem)


ROW_UNROLL = 8


def _scatter_kernel(pad_ref, dest_ref, h_ref, xs_ref, zero_scr, sem):
    tm = h_ref.shape[0]
    zr = zero_scr.shape[0]

    @pl.when(pl.program_id(0) == 0)
    def _():
        zero_scr[...] = jnp.zeros_like(zero_scr)
        tail = [xs_ref.shape[0] - (j + 1) * zr for j in range(N_EXPERTS + 1)]
        starts = [pl.multiple_of(pad_ref[e], 8) for e in range(N_EXPERTS)]
        for start in tail:
            pltpu.make_async_copy(zero_scr, xs_ref.at[pl.ds(start, zr)], sem).start()
        for start in tail:
            pltpu.make_async_copy(zero_scr, xs_ref.at[pl.ds(start, zr)], sem).wait()
        for start in starts:
            pltpu.make_async_copy(zero_scr, xs_ref.at[pl.ds(start, zr)], sem).start()
        for start in starts:
            pltpu.make_async_copy(zero_scr, xs_ref.at[pl.ds(start, zr)], sem).wait()

    def issue(j, carry):
        for k in range(ROW_UNROLL):
            r = j * ROW_UNROLL + k
            _row_copy(h_ref, r, xs_ref, dest_ref[0, 0, r], sem).start(priority=0)
            _row_copy(h_ref, r, xs_ref, dest_ref[0, 0, tm + r], sem).start(priority=1)
        return carry

    lax.fori_loop(0, tm // ROW_UNROLL, issue, 0)
    for _ in range(2):
        pltpu.make_async_copy(h_ref, xs_ref.at[pl.ds(0, tm)], sem).wait()


def _scatter_rows(h_flat, dest, pad_start, n_rows, tm, zr):
    t, d = h_flat.shape
    nt = t // tm
    dest_blk = jnp.concatenate([dest[0].reshape(nt, 1, tm), dest[1].reshape(nt, 1, tm)], axis=2)
    return pl.pallas_call(
        _scatter_kernel,
        grid=(nt,),
        in_specs=[pl.BlockSpec(memory_space=pltpu.SMEM),
                  pl.BlockSpec((1, 1, 2 * tm), lambda i: (i, 0, 0), memory_space=pltpu.SMEM),
                  pl.BlockSpec((tm, d), lambda i: (i, 0))],
        out_specs=pl.BlockSpec(memory_space=pl.ANY),
        out_shape=jax.ShapeDtypeStruct((n_rows, d), F32),
        scratch_shapes=[pltpu.VMEM((zr, d), F32), pltpu.SemaphoreType.DMA],
        compiler_params=_cp(("arbitrary",), VMEM_LIMIT),
        name="moe_scatter",
    )(pad_start, dest_blk, h_flat)


def _experts_kernel(te_ref, nv_ref, x_ref, w1_ref, w3_ref, w2_ref, o_ref, h_scr, acc_scr):
    del te_ref
    i = pl.program_id(0)
    f = pl.program_id(1)
    live = i < nv_ref[0]

    @pl.when(jnp.logical_and(live, f == 0))
    def _():
        h_scr[...] = x_ref[...].astype(BF16)
        acc_scr[...] = jnp.zeros_like(acc_scr)

    @pl.when(live)
    def _():
        h = h_scr[...]
        act = _silu(_dot(h, w1_ref[0])) * _dot(h, w3_ref[0])
        acc_scr[...] += _dot(act.astype(BF16), w2_ref[0])

    @pl.when(f == pl.num_programs(1) - 1)
    def _():
        o_ref[...] = jnp.where(live, acc_scr[...], 0.0)


def _experts(xs, tile_expert, n_valid, w1, w3, w2, tm, tf):
    p, d = xs.shape
    ff = w1.shape[2]
    nt = p // tm
    row = lambda i, f, te, nv: (jnp.minimum(i, nv[0] - 1), 0)
    grid_spec = pltpu.PrefetchScalarGridSpec(
        num_scalar_prefetch=2,
        grid=(nt, ff // tf),
        in_specs=[pl.BlockSpec((tm, d), row),
                  pl.BlockSpec((1, d, tf), lambda i, f, te, nv: (te[i], 0, f)),
                  pl.BlockSpec((1, d, tf), lambda i, f, te, nv: (te[i], 0, f)),
                  pl.BlockSpec((1, tf, d), lambda i, f, te, nv: (te[i], f, 0))],
        out_specs=pl.BlockSpec((tm, d), lambda i, f, te, nv: (i, 0)),
        scratch_shapes=[pltpu.VMEM((tm, d), BF16), pltpu.VMEM((tm, d), F32)])
    return pl.pallas_call(
        _experts_kernel,
        grid_spec=grid_spec,
        out_shape=jax.ShapeDtypeStruct((p, d), F32),
        compiler_params=_cp(("arbitrary", "arbitrary"), VMEM_LIMIT),
        name="moe_experts",
    )(tile_expert, n_valid, xs, w1, w3, w2)


def _combine_kernel(dest_ref, meta_ref, x_ref, mg_ref, gf_ref, ys_ref, o_ref, ybuf, sem, *, final):
    tm = x_ref.shape[1]

    def issue(j, carry):
        for k in range(ROW_UNROLL):
            r = j * ROW_UNROLL + k
            _row_copy(ys_ref, dest_ref[0, 0, r], ybuf.at[0], r, sem).start(priority=0)
            _row_copy(ys_ref, dest_ref[0, 0, tm + r], ybuf.at[1], r, sem).start(priority=1)
        return carry

    lax.fori_loop(0, tm // ROW_UNROLL, issue, 0)
    for slot in range(2):
        pltpu.make_async_copy(ys_ref.at[pl.ds(0, tm)], ybuf.at[slot], sem).wait()
    meta = meta_ref[0]
    y = x_ref[0] + mg_ref[0] * (meta[:, 4:5] * ybuf[0] + meta[:, 5:6] * ybuf[1])
    if final:
        y = y * lax.rsqrt(jnp.mean(y * y, axis=-1, keepdims=True) + EPS) * gf_ref[...]
    o_ref[0] = y


def _combine(ys, dest, meta, x, mgate, g_final, tm):
    b, l, d = x.shape
    nt = l // tm
    final = g_final is not None
    gf = g_final if final else jnp.ones((1, d), F32)
    dest_blk = jnp.concatenate([dest[0].reshape(b * nt, 1, tm), dest[1].reshape(b * nt, 1, tm)], axis=2)
    return pl.pallas_call(
        functools.partial(_combine_kernel, final=final),
        grid=(b, nt),
        in_specs=[pl.BlockSpec((1, 1, 2 * tm), lambda bi, i: (bi * nt + i, 0, 0), memory_space=pltpu.SMEM),
                  pl.BlockSpec((1, tm, 8), lambda bi, i: (bi, i, 0)),
                  pl.BlockSpec((1, tm, d), lambda bi, i: (bi, i, 0)),
                  pl.BlockSpec((1, 1, d), lambda bi, i: (bi, 0, 0)),
                  pl.BlockSpec((1, d), lambda bi, i: (0, 0)),
                  pl.BlockSpec(memory_space=pl.ANY)],
        out_specs=pl.BlockSpec((1, tm, d), lambda bi, i: (bi, i, 0)),
        out_shape=jax.ShapeDtypeStruct((b, l, d), F32),
        scratch_shapes=[pltpu.VMEM((2, tm, d), F32), pltpu.SemaphoreType.DMA],
        compiler_params=_cp(("arbitrary", "arbitrary"), VMEM_LIMIT),
        name="moe_combine",
    )(dest_blk, meta, x, mgate, gf, ys)


def _moe_ffn(x, g, shift, scale, mgate, wr, br, w1, w3, w2, g_final, tm_e, tf):
    b, l, d = x.shape
    t = b * l
    ne = w1.shape[0]
    h, meta, cnt = _route(x, g, shift, scale, wr, br, min(512, l))
    e_idx = jnp.stack([meta[..., 0], meta[..., 1]]).astype(jnp.int32).reshape(2, t)
    rank = jnp.stack([meta[..., 2], meta[..., 3]]).astype(jnp.int32).reshape(2, t)
    counts = cnt[0, :ne].astype(jnp.int32)
    padded = ((counts + tm_e - 1) // tm_e) * tm_e
    ends = jnp.cumsum(padded)
    first_row = ends - padded
    dest = rank
    for e in range(ne):
        dest = dest + jnp.where(e_idx == e, first_row[e], 0)
    n_tiles = -(-2 * t // tm_e) + ne + 1
    starts = jnp.arange(n_tiles, dtype=jnp.int32) * tm_e
    tile_expert = jnp.minimum(jnp.sum(starts[:, None] >= ends[None, :], axis=1), ne - 1).astype(jnp.int32)
    n_valid = (ends[-1] // tm_e).astype(jnp.int32).reshape(1)
    pad_start = ((first_row + counts) // 8) * 8
    xs = _scatter_rows(h.reshape(t, d), dest, pad_start, n_tiles * tm_e, min(512, l), tm_e)
    ys = _experts(xs, tile_expert, n_valid, w1, w3, w2, tm_e, tf)
    return _combine(ys, dest, meta, x, mgate, g_final, min(512, l))


def _rope_tables(n_lat):
    rows = n_lat // GRID_W
    row_id = jnp.broadcast_to(jnp.arange(rows)[:, None], (rows, GRID_W)).reshape(-1).astype(F32)
    col_id = jnp.broadcast_to(jnp.arange(GRID_W)[None, :], (rows, GRID_W)).reshape(-1).astype(F32)
    q4 = RET_DH // 4
    inv = 1.0 / (ROPE_BASE ** (jnp.arange(q4, dtype=F32) / q4))
    ang_r = row_id[:, None] * inv[None, :]
    ang_c = col_id[:, None] * inv[None, :]
    zeros = jnp.zeros_like(ang_r)
    cos_h = jnp.concatenate([jnp.cos(ang_r)] * 2 + [jnp.cos(ang_c)] * 2, axis=-1)
    sin_a = jnp.concatenate([-jnp.sin(ang_r), zeros, -jnp.sin(ang_c), zeros], axis=-1)
    sin_b = jnp.concatenate([zeros, jnp.sin(ang_r), zeros, jnp.sin(ang_c)], axis=-1)
    tile = lambda t: jnp.tile(t, (1, RET_HEADS))
    return tile(cos_h), tile(sin_a), tile(sin_b)


def _zero_states(b):
    return (jnp.zeros((b, 3, LANES, LANES), F32), jnp.zeros((b, 2, LANES, LANES), F32),
            jnp.zeros((b, 2, 1, LANES), F32), jnp.zeros((b, 8, LANES), F32))


def kernel(x, c, ctx, c_ctx, w_mod, b_mod, g_mix, g_ffn, w_in, b_in, hy_conv, hy_f1, hy_fb1, hy_f2,
           hy_fb2, hy_f3, hy_decay, hy_skip, ret_decay, ml_conv, ml_gate_bias, w_up, w_out, ffn_w1,
           ffn_w3, ffn_w2, moe_router, moe_router_b, moe_w1, moe_w3, moe_w2, g_final):
    b, l, d = x.shape
    lc = ctx.shape[1]
    depth = w_mod.shape[0]
    hy_off, ret_off = 0, 3 * HY_W
    ml_off = ret_off + 4 * RET_W
    mlg_off = ml_off + 4 * ML_W
    gate_off = mlg_off + 4 * ML_HEADS

    mp = ((b + 1 + 7) // 8) * 8
    c_all = jnp.zeros((mp, d), F32).at[:b].set(c).at[b].set(c_ctx)
    mods = _mod(c_all, w_mod, b_mod).reshape(depth, mp, 6, d)

    rope = _rope_tables(l)
    tf_l, tf_c = min(256, l), min(256, lc)
    fwd_l = _dft_matrix(l, tf_l)
    fwd_c = _dft_matrix(lc, tf_c)
    tm_l = min(1024, l)
    tm_c = min(1024, b * lc)

    xl, xc = x, ctx
    for layer in range(depth):
        last = layer == depth - 1
        ml_ = mods[layer, :b][:, :, None, :]
        mc_ = mods[layer, b][None, :, None, :]
        wi = w_in[layer]
        bi = b_in[layer]
        w_main = jnp.concatenate(
            [wi[:, gate_off:], wi[:, ret_off:ml_off], wi[:, ml_off:mlg_off]], axis=1).astype(BF16)
        b_main = jnp.concatenate([bi[gate_off:], bi[ret_off:ml_off], bi[ml_off:mlg_off]])[None, :]
        w_hyt = wi[:, hy_off:ret_off].T.astype(BF16)
        b_hy = bi[hy_off:ret_off][:, None]
        w_g = jnp.zeros((d, LANES), F32).at[:, :4 * ML_HEADS].set(wi[:, mlg_off:gate_off]).astype(BF16)
        b_g = jnp.zeros((1, LANES), F32).at[0, :4 * ML_HEADS].set(bi[mlg_off:gate_off])
        gmix = g_mix[layer][None, :]

        um_l, uh_l, ug_l = _in_proj(xl, gmix, ml_[:, 0], ml_[:, 1], w_main, b_main, w_hyt, b_hy,
                                    w_g, b_g, tm_l)
        xc_flat = xc.reshape(1, b * lc, d)
        um_c, uh_c, ug_c = _in_proj(xc_flat, gmix, mc_[:, 0], mc_[:, 1], w_main, b_main, w_hyt, b_hy,
                                    w_g, b_g, tm_c)
        um_c = um_c.reshape(b, lc, N_MAIN)
        ug_c = ug_c.reshape(b, lc, LANES)

        lg = jax.nn.log_sigmoid(ret_decay[layer].astype(F32))
        gbias = jnp.zeros((1, LANES), F32).at[0, :4 * ML_HEADS].set(ml_gate_bias[layer].reshape(-1))
        qk_c = _mlqk(um_c, ml_conv[layer])
        qk_l = _mlqk(um_l, ml_conv[layer])
        z0 = _zero_states(b)
        pf_c, sf = _scan(um_c, ug_c, qk_c, None, lg[0], gbias, z0, None, False)
        yc_rm, sb = _scan(um_c, ug_c, qk_c, None, lg[1], gbias, z0, pf_c, True)
        pf_l, _ = _scan(um_l, ug_l, qk_l, rope, lg[0], gbias, sf, None, False)
        yl_rm, _ = _scan(um_l, ug_l, qk_l, rope, lg[1], gbias, sb, pf_l, True)

        w_conv_t = hy_conv[layer].T
        skip = hy_skip[layer][:, :, None]

        def hyena(uh, length, fwd, bt):
            filt = _hyena_filters(length, hy_f1[layer], hy_fb1[layer], hy_f2[layer], hy_fb2[layer],
                                  hy_f3[layer], hy_decay[layer])
            coef = _filter_coefs(filt, fwd)
            z1 = _hyena_conv(uh, 0, uh, 1, w_conv_t, skip[0], fwd, coef[0], True, bt)
            return _hyena_conv(z1, 0, uh, 2, w_conv_t, skip[1], fwd, coef[1], False, bt)

        yh_l = hyena(uh_l, l, fwd_l, 1)
        wup = w_up[layer].astype(BF16)
        wout = w_out[layer].astype(BF16)
        xl = _merge(yh_l, yl_rm[0], yl_rm[1], um_l, xl, ml_[:, 2], wup, wout, min(512, l))
        if not last:
            uh_cb = uh_c.reshape(3 * HY_W, b, lc).transpose(1, 0, 2)
            yh_c = hyena(uh_cb, lc, fwd_c, b)
            xc = _merge(yh_c, yc_rm[0], yc_rm[1], um_c, xc, mc_[:, 2], wup, wout, min(512, lc))

        j = layer // 2
        gffn = g_ffn[layer][None, :]
        gfin = g_final[None, :] if last else None
        xcf = xc.reshape(1, b * lc, d)
        if layer % 2 == 0:
            w1 = ffn_w1[j].astype(BF16)
            w3 = ffn_w3[j].astype(BF16)
            w2 = ffn_w2[j].astype(BF16)
            tf = _ff_tile(w1.shape[1])
            xl = _ffn(xl, gffn, ml_[:, 3], ml_[:, 4], ml_[:, 5], w1, w3, w2, gfin, min(1024, l), tf)
            if not last:
                xcf = _ffn(xcf, gffn, mc_[:, 3], mc_[:, 4], mc_[:, 5], w1, w3, w2, None, min(512, b * lc), tf)
        else:
            w1 = moe_w1[j].astype(BF16)
            w3 = moe_w3[j].astype(BF16)
            w2 = moe_w2[j].astype(BF16)
            wr = jnp.zeros((d, LANES), F32).at[:, :N_EXPERTS].set(moe_router[j])
            br = jnp.zeros((1, LANES), F32).at[0, :N_EXPERTS].set(moe_router_b[j])
            tf = _ff_tile(w1.shape[2])
            xl = _moe_ffn(xl, gffn, ml_[:, 3], ml_[:, 4], ml_[:, 5], wr, br, w1, w3, w2, gfin, MOE_ROWS, tf)
            if not last:
                xcf = _moe_ffn(xcf, gffn, mc_[:, 3], mc_[:, 4], mc_[:, 5], wr, br, w1, w3, w2, None, MOE_ROWS, tf)
        xc = xcf.reshape(b, lc, d)
    if depth == 0:
        raise ValueError("depth must be positive")
    return xl
```

```python
import functools
import math

import jax
import jax.numpy as jnp
from jax import lax
from jax.experimental import pallas as pl
from jax.experimental.pallas import tpu as pltpu

F32 = jnp.float32
BF16 = jnp.bfloat16
EPS = 1e-6

GRID_W = 64
HY_W = 384
RET_HEADS, RET_DH = 6, 64
RET_W = RET_HEADS * RET_DH
ML_HEADS, ML_DH = 4, 64
ML_W = ML_HEADS * ML_DH
HY_EMB = 33
ROPE_BASE = 10000.0
N_EXPERTS = 8
LANES = 128

GATE_OFF = 0
RETM_OFF = 3 * 1024
MLM_OFF = RETM_OFF + 4 * RET_W
N_MAIN = MLM_OFF + 4 * ML_W
N_TILE = 2816
CHUNK = 256

MOE_ROWS = 1024

VMEM_LIMIT = 56 * 1024 * 1024


def _ff_tile(ff):
    return ff // 2 if (ff // 2) % LANES == 0 else ff


def _cp(sem, vmem=None):
    return pltpu.CompilerParams(dimension_semantics=sem, vmem_limit_bytes=vmem)


def _split3(a):
    hi = a.astype(BF16)
    r1 = a - hi.astype(F32)
    mid = r1.astype(BF16)
    lo = (r1 - mid.astype(F32)).astype(BF16)
    return hi, mid, lo


def _dot(a, b):
    return jnp.dot(a, b, preferred_element_type=F32)


def _dot_nt(a, b):
    return lax.dot_general(a, b, (((1,), (1,)), ((), ())), preferred_element_type=F32)


def _dot3(a, b):
    ah, am, al = _split3(a)
    bh, bm, bl = _split3(b)
    return (_dot(ah, bh) + (_dot(ah, bm) + _dot(am, bh))
            + (_dot(am, bm) + _dot(ah, bl) + _dot(al, bh)))


def _sigmoid(x):
    return 1.0 / (1.0 + jnp.exp(-x))


def _silu(x):
    return x * _sigmoid(x)


def _log_sigmoid(x):
    return jnp.minimum(x, 0.0) - jnp.log(1.0 + jnp.exp(-jnp.abs(x)))


def _mod_kernel(c_ref, w_ref, b_ref, o_ref):
    s = _silu(c_ref[...])
    o_ref[0] = _dot3(s, w_ref[0]) + b_ref[0]


def _mod(c_all, w_mod, b_mod):
    depth, d, n = w_mod.shape
    mp = c_all.shape[0]
    tn = 1536
    return pl.pallas_call(
        _mod_kernel,
        grid=(depth, n // tn),
        in_specs=[pl.BlockSpec((mp, d), lambda l, j: (0, 0)),
                  pl.BlockSpec((1, d, tn), lambda l, j: (l, 0, j)),
                  pl.BlockSpec((1, 1, tn), lambda l, j: (l, 0, j))],
        out_specs=pl.BlockSpec((1, mp, tn), lambda l, j: (l, 0, j)),
        out_shape=jax.ShapeDtypeStruct((depth, mp, n), F32),
        compiler_params=_cp(("parallel", "parallel"), VMEM_LIMIT),
        name="mod",
    )(c_all, w_mod, b_mod.reshape(depth, 1, n))


def _norm_mod(x, g, shift, scale):
    y = x * lax.rsqrt(jnp.mean(x * x, axis=-1, keepdims=True) + EPS)
    return (y * g) * (1.0 + scale) + shift


def _inproj_kernel(x_ref, g_ref, sh_ref, sc_ref, wm_ref, bm_ref, wh_ref, bh_ref, wg_ref, bg_ref,
                   um_ref, uh_ref, ug_ref, h_scr):
    n = pl.program_id(2)

    @pl.when(n == 0)
    def _():
        h = _norm_mod(x_ref[0], g_ref[...], sh_ref[0], sc_ref[0]).astype(BF16)
        h_scr[...] = h
        uh_ref[0] = _dot_nt(wh_ref[...], h) + bh_ref[...]
        ug_ref[0] = _dot(h, wg_ref[...]) + bg_ref[...]

    um_ref[0] = (_dot(h_scr[...], wm_ref[...]) + bm_ref[...]).astype(BF16)


def _in_proj(x, g, shift, scale, w_main, b_main, w_hyt, b_hy, w_g, b_g, tm):
    bx, lx, d = x.shape
    per_batch_mod = shift.shape[0] > 1
    mi = (lambda b, i, n: (b, 0, 0)) if per_batch_mod else (lambda b, i, n: (0, 0, 0))
    nh = w_hyt.shape[0]
    return pl.pallas_call(
        _inproj_kernel,
        grid=(bx, lx // tm, N_MAIN // N_TILE),
        in_specs=[pl.BlockSpec((1, tm, d), lambda b, i, n: (b, i, 0)),
                  pl.BlockSpec((1, d), lambda b, i, n: (0, 0)),
                  pl.BlockSpec((1, 1, d), mi),
                  pl.BlockSpec((1, 1, d), mi),
                  pl.BlockSpec((d, N_TILE), lambda b, i, n: (0, n)),
                  pl.BlockSpec((1, N_TILE), lambda b, i, n: (0, n)),
                  pl.BlockSpec((nh, d), lambda b, i, n: (0, 0)),
                  pl.BlockSpec((nh, 1), lambda b, i, n: (0, 0)),
                  pl.BlockSpec((d, LANES), lambda b, i, n: (0, 0)),
                  pl.BlockSpec((1, LANES), lambda b, i, n: (0, 0))],
        out_specs=[pl.BlockSpec((1, tm, N_TILE), lambda b, i, n: (b, i, n)),
                   pl.BlockSpec((1, nh, tm), lambda b, i, n: (b, 0, i)),
                   pl.BlockSpec((1, tm, LANES), lambda b, i, n: (b, i, 0))],
        out_shape=[jax.ShapeDtypeStruct((bx, lx, N_MAIN), BF16),
                   jax.ShapeDtypeStruct((bx, nh, lx), F32),
                   jax.ShapeDtypeStruct((bx, lx, LANES), F32)],
        scratch_shapes=[pltpu.VMEM((tm, d), BF16)],
        compiler_params=_cp(("parallel", "parallel", "arbitrary"), VMEM_LIMIT),
        name="in_proj",
    )(x, g, shift, scale, w_main, b_main, w_hyt, b_hy, w_g, b_g)


def _dft_kernel(h_ref, f_ref, o_ref):
    hh, hm, hl = _split3(h_ref[...])
    f = f_ref[0]
    o_ref[0] = _dot(hh, f) + _dot(hm, f) + _dot(hl, f)


def _dft(h_rows, fwd):
    r, l = h_rows.shape
    nf, _, tw = fwd.shape
    tr = HY_W if r % HY_W == 0 else r
    return pl.pallas_call(
        _dft_kernel,
        grid=(nf, r // tr),
        in_specs=[pl.BlockSpec((tr, l), lambda f, i: (i, 0)),
                  pl.BlockSpec((1, l, tw), lambda f, i: (f, 0, 0))],
        out_specs=pl.BlockSpec((1, tr, tw), lambda f, i: (f, i, 0)),
        out_shape=jax.ShapeDtypeStruct((nf, r, tw), F32),
        compiler_params=_cp(("parallel", "parallel"), VMEM_LIMIT),
        name="dft",
    )(h_rows, fwd)


def _short_conv_lanes(u, w):
    l = u.shape[1]
    t = lax.broadcasted_iota(jnp.int32, u.shape, 1)
    prev = jnp.where(t == 0, 0.0, pltpu.roll(u, 1, 1))
    nxt = jnp.where(t == l - 1, 0.0, pltpu.roll(u, l - 1, 1))
    return prev * w[:, 0:1] + u * w[:, 1:2] + nxt * w[:, 2:3]


def _hyena_kernel(in_ref, gate_ref, wi_ref, wg_ref, skip_ref, f_ref, coef_ref, o_ref,
                  inb_scr, y_scr, *rest, conv_in, bt, cw, tf):
    f = pl.program_id(1)
    nf = pl.num_programs(1)

    @pl.when(f == 0)
    def _():
        for i in range(bt):
            u = in_ref[i]
            if conv_in:
                u = _short_conv_lanes(u, wi_ref[...])
                rest[0][i * cw:(i + 1) * cw, :] = u
            inb_scr[i * cw:(i + 1) * cw, :] = u.astype(BF16)
        o_ref[...] = jnp.zeros_like(o_ref)

    fw = f_ref[0]
    spec = _dot(inb_scr[...], fw)
    ca, cb, cc, cd = coef_ref[0, 0], coef_ref[0, 1], coef_ref[0, 2], coef_ref[0, 3]
    for i in range(bt):
        ur = spec[i * cw:(i + 1) * cw, :tf]
        ui = spec[i * cw:(i + 1) * cw, tf:]
        y_scr[i * cw:(i + 1) * cw, :tf] = (ur * ca + ui * cb).astype(BF16)
        y_scr[i * cw:(i + 1) * cw, tf:] = (ur * cc + ui * cd).astype(BF16)
    back = _dot_nt(y_scr[...], fw)
    for i in range(bt):
        o_ref[i] += back[i * cw:(i + 1) * cw, :]

    @pl.when(f == nf - 1)
    def _():
        for i in range(bt):
            gate = _short_conv_lanes(gate_ref[i], wg_ref[...])
            u = rest[0][i * cw:(i + 1) * cw, :] if conv_in else in_ref[i]
            o_ref[i] = gate * (o_ref[i] + skip_ref[...] * u)


def _hyena_conv(src, src_blk, gate_src, gate_blk, w_conv_t, skip, fwd, coef, conv_in, bt):
    b, _, l = src.shape
    cw = HY_W
    nf, _, tw = fwd.shape
    tf = tw // 2
    kern = functools.partial(_hyena_kernel, conv_in=conv_in, bt=bt, cw=cw, tf=tf)
    scratch = [pltpu.VMEM((bt * cw, l), BF16), pltpu.VMEM((bt * cw, tw), BF16)]
    if conv_in:
        scratch += [pltpu.VMEM((bt * cw, l), F32)]
    return pl.pallas_call(
        kern,
        grid=(b // bt, nf),
        in_specs=[pl.BlockSpec((bt, cw, l), lambda i, f: (i, src_blk, 0)),
                  pl.BlockSpec((bt, cw, l), lambda i, f: (i, gate_blk, 0)),
                  pl.BlockSpec((cw, 3), lambda i, f: (src_blk if conv_in else 0, 0)),
                  pl.BlockSpec((cw, 3), lambda i, f: (gate_blk, 0)),
                  pl.BlockSpec((cw, 1), lambda i, f: (0, 0)),
                  pl.BlockSpec((1, l, tw), lambda i, f: (f, 0, 0)),
                  pl.BlockSpec((1, 4, cw, tf), lambda i, f: (f, 0, 0, 0))],
        out_specs=pl.BlockSpec((bt, cw, l), lambda i, f: (i, 0, 0)),
        out_shape=jax.ShapeDtypeStruct((b, cw, l), F32),
        scratch_shapes=scratch,
        compiler_params=_cp(("parallel", "arbitrary"), VMEM_LIMIT),
        name="hyena_conv",
    )(src, gate_src, w_conv_t, w_conv_t, skip, fwd, coef)


def _dft_matrix(l, tf):
    n = jnp.arange(l, dtype=jnp.int32)[:, None]
    k = jnp.arange(l, dtype=jnp.int32)[None, :]
    blk = 64 if l % 64 == 0 else 1
    hi = (jnp.arange(l // blk, dtype=jnp.int32)[:, None] * blk * k) % (2 * l)
    lo = (jnp.arange(blk, dtype=jnp.int32)[:, None] * k) % (2 * l)
    ang_hi = hi.astype(F32)[:, None, :] * (math.pi / l)
    ang_lo = lo.astype(F32)[None, :, :] * (math.pi / l)
    ch, sh, cl, sl = jnp.cos(ang_hi), jnp.sin(ang_hi), jnp.cos(ang_lo), jnp.sin(ang_lo)
    re = (ch * cl - sh * sl).reshape(l, l)
    im = -(sh * cl + ch * sl).reshape(l, l)
    nyq = jnp.where(n % 2 == 0, 1.0, -1.0).astype(F32)
    im = jnp.where(k == 0, nyq, im)
    nf = l // tf
    re = re.reshape(l, nf, tf)
    im = im.reshape(l, nf, tf)
    return jnp.concatenate([re, im], axis=-1).transpose(1, 0, 2).astype(BF16)


def _hyena_filters(length, f1, fb1, f2, fb2, f3, decay):
    hp = lax.Precision.HIGHEST
    pos = jnp.arange(length, dtype=F32)
    t = pos / max(length - 1, 1)
    n_bands = (HY_EMB - 1) // 2
    bands = jnp.linspace(1e-4, n_bands - 1, n_bands, dtype=F32)
    z = (2.0 * math.pi * pos / length)[:, None] * bands[None, :]
    feats = jnp.concatenate([t[:, None], jnp.cos(z), -jnp.sin(z)], axis=-1)
    hid = jnp.sin(jnp.dot(feats, f1, precision=hp) + fb1)
    hid = jnp.sin(jnp.dot(hid, f2, precision=hp) + fb2)
    h = jnp.dot(hid, f3, precision=hp)
    window = jnp.exp(-t[:, None] * jnp.abs(decay)[None, :])
    return h * window


def _filter_coefs(filt, fwd):
    l = filt.shape[0]
    nf, _, tw = fwd.shape
    tf = tw // 2
    rows = filt.T
    is_bwd = (jnp.arange(rows.shape[0]) // HY_W) % 2 == 1
    first = jnp.arange(l)[None, :] == 0
    rows = jnp.where(is_bwd[:, None] & first, 0.0, rows)
    spec = _dft(rows, fwd)
    spec = spec.reshape(nf, 2, 2, HY_W, 2, tf)
    sf, sb = spec[:, :, 0], spec[:, :, 1]
    kre = sf[..., 0, :] + sb[..., 0, :]
    kim = sf[..., 1, :] - sb[..., 1, :]
    knyq = sf[..., 1, :] + sb[..., 1, :]
    kbin = (jnp.arange(nf)[:, None] * tf + jnp.arange(tf)[None, :])
    is0 = (kbin == 0)[:, None, None, :]
    wgt = jnp.where(is0, 1.0, 2.0) / (2.0 * l)
    ca = kre * wgt
    cb = jnp.where(is0, 0.0, -kim) * wgt
    cc = jnp.where(is0, 0.0, kim) * wgt
    cd = jnp.where(is0, knyq, kre) * wgt
    coef = jnp.stack([ca, cb, cc, cd], axis=2)
    return coef.transpose(1, 0, 2, 3, 4)


def _mlqk_kernel(u_ref, w_ref, o_ref):
    u = u_ref[0].astype(F32)
    l = u.shape[0]
    t = lax.broadcasted_iota(jnp.int32, u.shape, 0)
    prev = jnp.where(t == 0, 0.0, pltpu.roll(u, 1, 0))
    nxt = jnp.where(t == l - 1, 0.0, pltpu.roll(u, l - 1, 0))
    y = _silu(prev * w_ref[0:1, :] + u * w_ref[1:2, :] + nxt * w_ref[2:3, :])
    lane = lax.broadcasted_iota(jnp.int32, u.shape, 1)
    o_ref[0] = jnp.where(lane >= ML_W, y * (ML_DH ** -0.5), y).astype(BF16)


def _mlqk(u_main, ml_conv):
    b, l, _ = u_main.shape
    w = 2 * ML_W
    return pl.pallas_call(
        _mlqk_kernel,
        grid=(b,),
        in_specs=[pl.BlockSpec((1, l, w), lambda i: (i, 0, MLM_OFF // w)),
                  pl.BlockSpec((3, w), lambda i: (0, 0))],
        out_specs=pl.BlockSpec((1, l, w), lambda i: (i, 0, 0)),
        out_shape=jax.ShapeDtypeStruct((b, l, w), BF16),
        compiler_params=_cp(("parallel",), VMEM_LIMIT),
        name="mlqk",
    )(u_main, ml_conv)


def _head_norm_pair(xb, lo):
    def halves(v):
        s_all = jnp.sum(v, axis=1, keepdims=True)
        s_lo = jnp.sum(jnp.where(lo, v, 0.0), axis=1, keepdims=True)
        return jnp.where(lo, s_lo, s_all - s_lo) * (1.0 / 64.0)
    xc = xb - halves(xb)
    return xc * lax.rsqrt(halves(xc * xc) + EPS)


def _scan_kernel(*refs, reverse, use_rope, c):
    it = iter(refs)
    lg_ref = next(it)
    ret_ref = next(it)
    if use_rope:
        cos_ref, sa_ref, sb_ref = next(it), next(it), next(it)
    mlqk_ref = next(it)
    mlvo_ref = next(it)
    mlg_ref = next(it)
    gb_ref = next(it)
    rs0_ref, mc0_ref, mn0_ref, mm0_ref = next(it), next(it), next(it), next(it)
    if reverse:
        of_ref, hf_ref = next(it), next(it)
    o_ref, h_ref = next(it), next(it)
    rs_out, mc_out, mn_out, mm_out = next(it), next(it), next(it), next(it)
    dec_scr, qd_scr, kd_scr, tri_scr, rs_scr, mc_scr, mn_scr, mm_scr = (
        next(it), next(it), next(it), next(it), next(it), next(it), next(it), next(it))

    ci = pl.program_id(1)
    nc = pl.num_programs(1)
    d = 1 if reverse else 0

    tt = lax.broadcasted_iota(jnp.int32, (c, c), 0)
    ss = lax.broadcasted_iota(jnp.int32, (c, c), 1)
    rel = (ss - tt) if reverse else (tt - ss)
    valid = rel >= 0
    lane = lax.broadcasted_iota(jnp.int32, (c, LANES), 1)
    lo = lane < 64
    trow = lax.broadcasted_iota(jnp.int32, (c, LANES), 0)
    r128 = lax.broadcasted_iota(jnp.int32, (LANES, LANES), 0)
    c128 = lax.broadcasted_iota(jnp.int32, (LANES, LANES), 1)
    bdiag = (r128 < 64) == (c128 < 64)
    lo_row = lax.broadcasted_iota(jnp.int32, (1, LANES), 1) < 64

    @pl.when(ci == 0)
    def _():
        relf = jnp.where(valid, rel, 0).astype(F32)
        qpow = ((c - trow) if reverse else (trow + 1)).astype(F32)
        kpow = (trow if reverse else (c - 1 - trow)).astype(F32)
        for h in range(RET_HEADS):
            dec_scr[h] = jnp.where(valid, jnp.exp(relf * lg_ref[h]), 0.0)
        for p in range(RET_HEADS // 2):
            lg_l = jnp.where(lo, lg_ref[2 * p], lg_ref[2 * p + 1])
            qd_scr[p] = jnp.exp(qpow * lg_l)
            kd_scr[p] = jnp.exp(kpow * lg_l)
        tri_scr[...] = jnp.where(valid, 1.0, 0.0).astype(BF16)
        rs_scr[...] = rs0_ref[0]
        mc_scr[...] = mc0_ref[0]
        mn_scr[...] = mn0_ref[0]
        mm_scr[...] = mm0_ref[0]

    u = ret_ref[0]
    for p in range(RET_HEADS // 2):
        sl = slice(p * LANES, (p + 1) * LANES)
        q = u[:, p * LANES:(p + 1) * LANES].astype(F32) * (RET_DH ** -0.5)
        k = u[:, RET_W + p * LANES:RET_W + (p + 1) * LANES].astype(F32)
        v = u[:, 2 * RET_W + p * LANES:2 * RET_W + (p + 1) * LANES]
        if use_rope:
            cs, sa, sb = cos_ref[:, sl], sa_ref[:, sl], sb_ref[:, sl]
            q = q * cs + pltpu.roll(q, LANES - 16, 1) * sa + pltpu.roll(q, 16, 1) * sb
            k = k * cs + pltpu.roll(k, LANES - 16, 1) * sa + pltpu.roll(k, 16, 1) * sb
        qb = q.astype(BF16)
        kb = k.astype(BF16)
        outs = []
        for hh in range(2):
            qm = jnp.where(lo == (hh == 0), qb, jnp.zeros_like(qb))
            s = _dot_nt(qm, kb) * dec_scr[2 * p + hh]
            outs.append(_dot(s.astype(BF16), v))
        o_intra = jnp.where(lo, outs[0], outs[1])
        st = rs_scr[p]
        o_inter = _dot(qb, st.astype(BF16)) * qd_scr[p]
        o_new = o_intra + o_inter
        kw_t = (k * kd_scr[p]).T.astype(BF16)
        upd = _dot(kw_t, v)
        cdec = jnp.exp(jnp.where(lo_row, lg_ref[2 * p], lg_ref[2 * p + 1]) * float(c))
        rs_scr[p] = st * cdec + jnp.where(bdiag, upd, 0.0)
        if reverse:
            o_tot = of_ref[0, :, sl] + o_new
            g = u[:, 3 * RET_W + p * LANES:3 * RET_W + (p + 1) * LANES].astype(F32)
            o_ref[0, :, sl] = _head_norm_pair(o_tot, lo) * _silu(g)
        else:
            o_ref[0, :, sl] = o_new

    gates = mlg_ref[0] + gb_ref[...]
    lf = _log_sigmoid(gates)
    tri = tri_scr[...]
    l_hi, l_mid, l_lo = _split3(lf)
    cum = _dot(tri, l_hi) + _dot(tri, l_mid) + _dot(tri, l_lo)
    cum_t = cum.T
    gates_t = gates.T
    last = 0 if reverse else c - 1
    qk = mlqk_ref[0]
    vo = mlvo_ref[0]
    for p in range(ML_HEADS // 2):
        sl = slice(p * LANES, (p + 1) * LANES)
        qb = qk[:, p * LANES:(p + 1) * LANES]
        kb = qk[:, ML_W + p * LANES:ML_W + (p + 1) * LANES]
        v = vo[:, p * LANES:(p + 1) * LANES]
        cst = mc_scr[p]
        nrow = mn_scr[p]
        inter_num = _dot(qb, cst.astype(BF16))
        qf = qb.astype(F32)
        kf = kb.astype(F32)
        qn = qf * nrow
        nums, dens, a_l, m_l, wk_l, as_l = [], [], [], [], [], []
        for hh in range(2):
            h = 2 * p + hh
            ic = d * 8 + h
            fc = d * 8 + 4 + h
            b_col = cum[:, fc:fc + 1]
            b_row = cum_t[fc:fc + 1, :]
            i_row = gates_t[ic:ic + 1, :]
            i_col = gates[:, ic:ic + 1]
            m_prev = mm_scr[h:h + 1, 0:1]
            dlog = jnp.where(valid, b_col - b_row + i_row, -jnp.inf)
            inter = b_col + m_prev
            m_t = jnp.maximum(inter, jnp.max(dlog, axis=1, keepdims=True))
            w = jnp.exp(dlog - m_t)
            a = jnp.exp(inter - m_t)
            qm = jnp.where(lo == (hh == 0), qb, jnp.zeros_like(qb))
            s = _dot_nt(qm, kb) * w
            nums.append(_dot(s.astype(BF16), v))
            qn_h = jnp.sum(jnp.where(lo == (hh == 0), qn, 0.0), axis=1, keepdims=True)
            dens.append(jnp.sum(s, axis=1, keepdims=True) + a * qn_h)
            a_l.append(a)
            m_l.append(m_t)
            total = cum[last:last + 1, fc:fc + 1]
            wlog = total - b_col + i_col
            m_new = jnp.maximum(total + m_prev, jnp.max(wlog, axis=0, keepdims=True))
            wk_l.append(jnp.exp(wlog - m_new))
            as_l.append(jnp.exp(total + m_prev - m_new))
            mm_scr[h:h + 1, :] = jnp.broadcast_to(m_new, (1, LANES))
        a_lanes = jnp.where(lo, a_l[0], a_l[1])
        num = jnp.where(lo, nums[0], nums[1]) + a_lanes * inter_num
        den = jnp.where(lo, dens[0], dens[1])
        m_lanes = jnp.where(lo, m_l[0], m_l[1])
        h_new = num / jnp.maximum(jnp.abs(den), jnp.exp(-m_lanes))
        kw = kf * jnp.where(lo, wk_l[0], wk_l[1])
        as_row = jnp.where(lo_row, as_l[0], as_l[1])
        upd = _dot(kw.T.astype(BF16), v)
        mc_scr[p] = cst * as_row + jnp.where(bdiag, upd, 0.0)
        mn_scr[p] = nrow * as_row + jnp.sum(kw, axis=0, keepdims=True)
        if reverse:
            h_tot = hf_ref[0, :, sl] + h_new
            og = vo[:, ML_W + p * LANES:ML_W + (p + 1) * LANES].astype(F32)
            h_ref[0, :, sl] = _sigmoid(og) * _head_norm_pair(h_tot, lo)
        else:
            h_ref[0, :, sl] = h_new

    @pl.when(ci == nc - 1)
    def _():
        rs_out[0] = rs_scr[...]
        mc_out[0] = mc_scr[...]
        mn_out[0] = mn_scr[...]
        mm_out[0] = mm_scr[...]


def _scan(u_main, u_gates, mlqk, rope, lg, gate_bias, states, prev, reverse):
    b, l, _ = u_main.shape
    c = min(CHUNK, l)
    nc = l // c
    use_rope = rope is not None
    cix = (lambda i: nc - 1 - i) if reverse else (lambda i: i)
    in_specs = [pl.BlockSpec(memory_space=pltpu.SMEM),
                pl.BlockSpec((1, c, 4 * RET_W), lambda bi, i: (bi, cix(i), RETM_OFF // (4 * RET_W)))]
    args = [lg, u_main]
    if use_rope:
        in_specs += [pl.BlockSpec((c, RET_W), lambda bi, i: (cix(i), 0))] * 3
        args += list(rope)
    in_specs += [pl.BlockSpec((1, c, 2 * ML_W), lambda bi, i: (bi, cix(i), 0)),
                 pl.BlockSpec((1, c, 2 * ML_W), lambda bi, i: (bi, cix(i), MLM_OFF // (2 * ML_W) + 1)),
                 pl.BlockSpec((1, c, LANES), lambda bi, i: (bi, cix(i), 0)),
                 pl.BlockSpec((1, LANES), lambda bi, i: (0, 0)),
                 pl.BlockSpec((1, 3, LANES, LANES), lambda bi, i: (bi, 0, 0, 0)),
                 pl.BlockSpec((1, 2, LANES, LANES), lambda bi, i: (bi, 0, 0, 0)),
                 pl.BlockSpec((1, 2, 1, LANES), lambda bi, i: (bi, 0, 0, 0)),
                 pl.BlockSpec((1, 8, LANES), lambda bi, i: (bi, 0, 0))]
    args += [mlqk, u_main, u_gates, gate_bias, *states]
    if reverse:
        in_specs += [pl.BlockSpec((1, c, RET_W), lambda bi, i: (bi, cix(i), 0)),
                     pl.BlockSpec((1, c, ML_W), lambda bi, i: (bi, cix(i), 0))]
        args += list(prev)
    out_specs = [pl.BlockSpec((1, c, RET_W), lambda bi, i: (bi, cix(i), 0)),
                 pl.BlockSpec((1, c, ML_W), lambda bi, i: (bi, cix(i), 0)),
                 pl.BlockSpec((1, 3, LANES, LANES), lambda bi, i: (bi, 0, 0, 0)),
                 pl.BlockSpec((1, 2, LANES, LANES), lambda bi, i: (bi, 0, 0, 0)),
                 pl.BlockSpec((1, 2, 1, LANES), lambda bi, i: (bi, 0, 0, 0)),
                 pl.BlockSpec((1, 8, LANES), lambda bi, i: (bi, 0, 0))]
    out_shape = [jax.ShapeDtypeStruct((b, l, RET_W), F32),
                 jax.ShapeDtypeStruct((b, l, ML_W), F32),
                 jax.ShapeDtypeStruct((b, 3, LANES, LANES), F32),
                 jax.ShapeDtypeStruct((b, 2, LANES, LANES), F32),
                 jax.ShapeDtypeStruct((b, 2, 1, LANES), F32),
                 jax.ShapeDtypeStruct((b, 8, LANES), F32)]
    scratch = [pltpu.VMEM((RET_HEADS, c, c), F32),
               pltpu.VMEM((3, c, LANES), F32),
               pltpu.VMEM((3, c, LANES), F32),
               pltpu.VMEM((c, c), BF16),
               pltpu.VMEM((3, LANES, LANES), F32),
               pltpu.VMEM((2, LANES, LANES), F32),
               pltpu.VMEM((2, 1, LANES), F32),
               pltpu.VMEM((8, LANES), F32)]
    kern = functools.partial(_scan_kernel, reverse=reverse, use_rope=use_rope, c=c)
    outs = pl.pallas_call(
        kern, grid=(b, nc), in_specs=in_specs, out_specs=out_specs, out_shape=out_shape,
        scratch_shapes=scratch,
        compiler_params=_cp(("parallel", "arbitrary"), VMEM_LIMIT),
        name="scan_bwd" if reverse else "scan_fwd",
    )(*args)
    return (outs[0], outs[1]), tuple(outs[2:])


def _merge_kernel(yh_ref, yr_ref, ym_ref, gt_ref, x_ref, mg_ref, wa_ref, wb_ref, wc_ref, wo_ref, o_ref):
    d = x_ref.shape[2]
    g = gt_ref[0].astype(F32)
    yh = yh_ref[0].T.astype(BF16)
    acc = _sigmoid(g[:, 0:d]) * _dot(yh, wa_ref[...])
    acc += _sigmoid(g[:, d:2 * d]) * _dot(yr_ref[0].astype(BF16), wb_ref[...])
    acc += _sigmoid(g[:, 2 * d:3 * d]) * _dot(ym_ref[0].astype(BF16), wc_ref[...])
    y = _dot(acc.astype(BF16), wo_ref[...])
    o_ref[0] = x_ref[0] + mg_ref[0] * y


def _merge(y_hyt, y_ret, y_ml, u_main, x, mgate, w_up, w_out, tm):
    b, l, d = x.shape
    per_batch_mod = mgate.shape[0] > 1
    mi = (lambda bi, i: (bi, 0, 0)) if per_batch_mod else (lambda bi, i: (0, 0, 0))
    wa, wb, wc = w_up[:HY_W], w_up[HY_W:HY_W + RET_W], w_up[HY_W + RET_W:]
    const = lambda bi, i: (0, 0)
    return pl.pallas_call(
        _merge_kernel,
        grid=(b, l // tm),
        in_specs=[pl.BlockSpec((1, HY_W, tm), lambda bi, i: (bi, 0, i)),
                  pl.BlockSpec((1, tm, RET_W), lambda bi, i: (bi, i, 0)),
                  pl.BlockSpec((1, tm, ML_W), lambda bi, i: (bi, i, 0)),
                  pl.BlockSpec((1, tm, 3 * d), lambda bi, i: (bi, i, 0)),
                  pl.BlockSpec((1, tm, d), lambda bi, i: (bi, i, 0)),
                  pl.BlockSpec((1, 1, d), mi),
                  pl.BlockSpec(wa.shape, const),
                  pl.BlockSpec(wb.shape, const),
                  pl.BlockSpec(wc.shape, const),
                  pl.BlockSpec(w_out.shape, const)],
        out_specs=pl.BlockSpec((1, tm, d), lambda bi, i: (bi, i, 0)),
        out_shape=jax.ShapeDtypeStruct((b, l, d), F32),
        compiler_params=_cp(("parallel", "parallel"), VMEM_LIMIT),
        name="merge",
    )(y_hyt, y_ret, y_ml, u_main, x, mgate, wa, wb, wc, w_out)


def _ffn_kernel(*refs, final):
    it = iter(refs)
    x_ref, g_ref, sh_ref, sc_ref, mg_ref = next(it), next(it), next(it), next(it), next(it)
    w1_ref, w3_ref, w2_ref = next(it), next(it), next(it)
    if final:
        gf_ref = next(it)
    o_ref = next(it)
    h_scr, acc_scr = next(it), next(it)
    f = pl.program_id(2)

    @pl.when(f == 0)
    def _():
        h = _norm_mod(x_ref[0], g_ref[...], sh_ref[0], sc_ref[0])
        h_scr[...] = h.astype(BF16)
        acc_scr[...] = jnp.zeros_like(acc_scr)

    h = h_scr[...]
    act = _silu(_dot(h, w1_ref[...])) * _dot(h, w3_ref[...])
    acc_scr[...] += _dot(act.astype(BF16), w2_ref[...])

    @pl.when(f == pl.num_programs(2) - 1)
    def _():
        y = x_ref[0] + mg_ref[0] * acc_scr[...]
        if final:
            y = y * lax.rsqrt(jnp.mean(y * y, axis=-1, keepdims=True) + EPS) * gf_ref[...]
        o_ref[0] = y


def _ffn(x, g, shift, scale, mgate, w1, w3, w2, g_final, tm, tf):
    b, l, d = x.shape
    ff = w1.shape[1]
    final = g_final is not None
    per_batch_mod = shift.shape[0] > 1
    mi = (lambda bi, i, f: (bi, 0, 0)) if per_batch_mod else (lambda bi, i, f: (0, 0, 0))
    const2 = lambda bi, i, f: (0, 0)
    in_specs = [pl.BlockSpec((1, tm, d), lambda bi, i, f: (bi, i, 0)),
                pl.BlockSpec((1, d), const2),
                pl.BlockSpec((1, 1, d), mi), pl.BlockSpec((1, 1, d), mi), pl.BlockSpec((1, 1, d), mi),
                pl.BlockSpec((d, tf), lambda bi, i, f: (0, f)),
                pl.BlockSpec((d, tf), lambda bi, i, f: (0, f)),
                pl.BlockSpec((tf, d), lambda bi, i, f: (f, 0))]
    args = [x, g, shift, scale, mgate, w1, w3, w2]
    if final:
        in_specs += [pl.BlockSpec((1, d), const2)]
        args += [g_final]
    return pl.pallas_call(
        functools.partial(_ffn_kernel, final=final),
        grid=(b, l // tm, ff // tf),
        in_specs=in_specs,
        out_specs=pl.BlockSpec((1, tm, d), lambda bi, i, f: (bi, i, 0)),
        out_shape=jax.ShapeDtypeStruct((b, l, d), F32),
        scratch_shapes=[pltpu.VMEM((tm, d), BF16), pltpu.VMEM((tm, d), F32)],
        compiler_params=_cp(("parallel", "parallel", "arbitrary"), VMEM_LIMIT),
        name="ffn",
    )(*args)


def _top2(logits):
    lane = lax.broadcasted_iota(jnp.int32, logits.shape, 1)
    neg = jnp.float32(-jnp.inf)
    logits = jnp.where(lane < N_EXPERTS, logits, neg)
    m1 = jnp.max(logits, axis=1, keepdims=True)
    i1 = jnp.min(jnp.where(logits == m1, lane, LANES), axis=1, keepdims=True)
    rest = jnp.where(lane == i1, neg, logits)
    m2 = jnp.max(rest, axis=1, keepdims=True)
    i2 = jnp.min(jnp.where(rest == m2, lane, LANES), axis=1, keepdims=True)
    e2 = jnp.exp(m2 - m1)
    return lane, i1, i2, 1.0 / (1.0 + e2), e2 / (1.0 + e2)


def _route_kernel(x_ref, g_ref, sh_ref, sc_ref, wr_ref, br_ref, h_ref, meta_ref, cnt_ref, cnt_scr):
    first = jnp.logical_and(pl.program_id(0) == 0, pl.program_id(1) == 0)

    @pl.when(first)
    def _():
        cnt_scr[...] = jnp.zeros_like(cnt_scr)

    h = _norm_mod(x_ref[0], g_ref[...], sh_ref[0], sc_ref[0])
    h_ref[0] = h
    tm = h.shape[0]
    lane, i1, i2, w1, w2 = _top2(_dot3(h, wr_ref[...]) + br_ref[...])
    oh1 = (lane == i1).astype(F32)
    oh2 = (lane == i2).astype(F32)
    both = oh1 + oh2
    tt = lax.broadcasted_iota(jnp.int32, (tm, tm), 0)
    ss = lax.broadcasted_iota(jnp.int32, (tm, tm), 1)
    before = jnp.where(tt > ss, 1.0, 0.0).astype(BF16)
    pref = _dot(before, both.astype(BF16)) + cnt_scr[...]
    r1 = jnp.sum(pref * oh1, axis=1, keepdims=True)
    r2 = jnp.sum(pref * oh2, axis=1, keepdims=True)
    cnt_scr[...] += jnp.sum(both, axis=0, keepdims=True)
    cnt_ref[...] = cnt_scr[...]
    vals = (i1.astype(F32), i2.astype(F32), r1, r2, w1, w2)
    meta = jnp.zeros((tm, LANES), F32)
    for j, val in enumerate(vals):
        meta = jnp.where(lane == j, val, meta)
    meta_ref[0] = meta[:, :8]


def _route(x, g, shift, scale, wr, br, tm):
    b, l, d = x.shape
    const2 = lambda bi, i: (0, 0)
    return pl.pallas_call(
        _route_kernel,
        grid=(b, l // tm),
        in_specs=[pl.BlockSpec((1, tm, d), lambda bi, i: (bi, i, 0)),
                  pl.BlockSpec((1, d), const2),
                  pl.BlockSpec((1, 1, d), lambda bi, i: (bi, 0, 0)),
                  pl.BlockSpec((1, 1, d), lambda bi, i: (bi, 0, 0)),
                  pl.BlockSpec((d, LANES), const2),
                  pl.BlockSpec((1, LANES), const2)],
        out_specs=[pl.BlockSpec((1, tm, d), lambda bi, i: (bi, i, 0)),
                   pl.BlockSpec((1, tm, 8), lambda bi, i: (bi, i, 0)),
                   pl.BlockSpec((1, LANES), const2)],
        out_shape=[jax.ShapeDtypeStruct((b, l, d), F32),
                   jax.ShapeDtypeStruct((b, l, 8), F32),
                   jax.ShapeDtypeStruct((1, LANES), F32)],
        scratch_shapes=[pltpu.VMEM((1, LANES), F32)],
        compiler_params=_cp(("arbitrary", "arbitrary"), VMEM_LIMIT),
        name="route",
    )(x, g, shift, scale, wr, br)


def _row_copy(src_ref, src_row, dst_ref, dst_row, sem):
    return pltpu.make_async_copy(src_ref.at[pl.ds(src_row, 1)], dst_ref.at[pl.ds(dst_row, 1)], sem)


ROW_UNROLL = 8


def _scatter_kernel(pad_ref, dest_ref, h_ref, xs_ref, zero_scr, sem):
    tm = h_ref.shape[0]
    zr = zero_scr.shape[0]

    @pl.when(pl.program_id(0) == 0)
    def _():
        zero_scr[...] = jnp.zeros_like(zero_scr)
        tail = [xs_ref.shape[0] - (j + 1) * zr for j in range(N_EXPERTS + 1)]
        starts = [pl.multiple_of(pad_ref[e], 8) for e in range(N_EXPERTS)]
        for start in tail:
            pltpu.make_async_copy(zero_scr, xs_ref.at[pl.ds(start, zr)], sem).start()
        for start in tail:
            pltpu.make_async_copy(zero_scr, xs_ref.at[pl.ds(start, zr)], sem).wait()
        for start in starts:
            pltpu.make_async_copy(zero_scr, xs_ref.at[pl.ds(start, zr)], sem).start()
        for start in starts:
            pltpu.make_async_copy(zero_scr, xs_ref.at[pl.ds(start, zr)], sem).wait()

    def issue(j, carry):
        for k in range(ROW_UNROLL):
            r = j * ROW_UNROLL + k
            _row_copy(h_ref, r, xs_ref, dest_ref[0, 0, r], sem).start(priority=0)
            _row_copy(h_ref, r, xs_ref, dest_ref[0, 0, tm + r], sem).start(priority=1)
        return carry

    lax.fori_loop(0, tm // ROW_UNROLL, issue, 0)
    for _ in range(2):
        pltpu.make_async_copy(h_ref, xs_ref.at[pl.ds(0, tm)], sem).wait()


def _scatter_rows(h_flat, dest, pad_start, n_rows, tm, zr):
    t, d = h_flat.shape
    nt = t // tm
    dest_blk = jnp.concatenate([dest[0].reshape(nt, 1, tm), dest[1].reshape(nt, 1, tm)], axis=2)
    return pl.pallas_call(
        _scatter_kernel,
        grid=(nt,),
        in_specs=[pl.BlockSpec(memory_space=pltpu.SMEM),
                  pl.BlockSpec((1, 1, 2 * tm), lambda i: (i, 0, 0), memory_space=pltpu.SMEM),
                  pl.BlockSpec((tm, d), lambda i: (i, 0))],
        out_specs=pl.BlockSpec(memory_space=pl.ANY),
        out_shape=jax.ShapeDtypeStruct((n_rows, d), F32),
        scratch_shapes=[pltpu.VMEM((zr, d), F32), pltpu.SemaphoreType.DMA],
        compiler_params=_cp(("arbitrary",), VMEM_LIMIT),
        name="moe_scatter",
    )(pad_start, dest_blk, h_flat)


def _experts_kernel(te_ref, nv_ref, x_ref, w1_ref, w3_ref, w2_ref, o_ref, h_scr, acc_scr):
    del te_ref
    i = pl.program_id(0)
    f = pl.program_id(1)
    live = i < nv_ref[0]

    @pl.when(jnp.logical_and(live, f == 0))
    def _():
        h_scr[...] = x_ref[...].astype(BF16)
        acc_scr[...] = jnp.zeros_like(acc_scr)

    @pl.when(live)
    def _():
        h = h_scr[...]
        act = _silu(_dot(h, w1_ref[0])) * _dot(h, w3_ref[0])
        acc_scr[...] += _dot(act.astype(BF16), w2_ref[0])

    @pl.when(f == pl.num_programs(1) - 1)
    def _():
        o_ref[...] = jnp.where(live, acc_scr[...], 0.0)


def _experts(xs, tile_expert, n_valid, w1, w3, w2, tm, tf):
    p, d = xs.shape
    ff = w1.shape[2]
    nt = p // tm
    row = lambda i, f, te, nv: (jnp.minimum(i, nv[0] - 1), 0)
    grid_spec = pltpu.PrefetchScalarGridSpec(
        num_scalar_prefetch=2,
        grid=(nt, ff // tf),
        in_specs=[pl.BlockSpec((tm, d), row),
                  pl.BlockSpec((1, d, tf), lambda i, f, te, nv: (te[i], 0, f)),
                  pl.BlockSpec((1, d, tf), lambda i, f, te, nv: (te[i], 0, f)),
                  pl.BlockSpec((1, tf, d), lambda i, f, te, nv: (te[i], f, 0))],
        out_specs=pl.BlockSpec((tm, d), lambda i, f, te, nv: (i, 0)),
        scratch_shapes=[pltpu.VMEM((tm, d), BF16), pltpu.VMEM((tm, d), F32)])
    return pl.pallas_call(
        _experts_kernel,
        grid_spec=grid_spec,
        out_shape=jax.ShapeDtypeStruct((p, d), F32),
        compiler_params=_cp(("arbitrary", "arbitrary"), VMEM_LIMIT),
        name="moe_experts",
    )(tile_expert, n_valid, xs, w1, w3, w2)


def _combine_kernel(dest_ref, meta_ref, x_ref, mg_ref, gf_ref, ys_ref, o_ref, ybuf, sem, *, final):
    tm = x_ref.shape[1]

    def issue(j, carry):
        for k in range(ROW_UNROLL):
            r = j * ROW_UNROLL + k
            _row_copy(ys_ref, dest_ref[0, 0, r], ybuf.at[0], r, sem).start(priority=0)
            _row_copy(ys_ref, dest_ref[0, 0, tm + r], ybuf.at[1], r, sem).start(priority=1)
        return carry

    lax.fori_loop(0, tm // ROW_UNROLL, issue, 0)
    for slot in range(2):
        pltpu.make_async_copy(ys_ref.at[pl.ds(0, tm)], ybuf.at[slot], sem).wait()
    meta = meta_ref[0]
    y = x_ref[0] + mg_ref[0] * (meta[:, 4:5] * ybuf[0] + meta[:, 5:6] * ybuf[1])
    if final:
        y = y * lax.rsqrt(jnp.mean(y * y, axis=-1, keepdims=True) + EPS) * gf_ref[...]
    o_ref[0] = y


def _combine(ys, dest, meta, x, mgate, g_final, tm):
    b, l, d = x.shape
    nt = l // tm
    final = g_final is not None
    gf = g_final if final else jnp.ones((1, d), F32)
    dest_blk = jnp.concatenate([dest[0].reshape(b * nt, 1, tm), dest[1].reshape(b * nt, 1, tm)], axis=2)
    return pl.pallas_call(
        functools.partial(_combine_kernel, final=final),
        grid=(b, nt),
        in_specs=[pl.BlockSpec((1, 1, 2 * tm), lambda bi, i: (bi * nt + i, 0, 0), memory_space=pltpu.SMEM),
                  pl.BlockSpec((1, tm, 8), lambda bi, i: (bi, i, 0)),
                  pl.BlockSpec((1, tm, d), lambda bi, i: (bi, i, 0)),
                  pl.BlockSpec((1, 1, d), lambda bi, i: (bi, 0, 0)),
                  pl.BlockSpec((1, d), lambda bi, i: (0, 0)),
                  pl.BlockSpec(memory_space=pl.ANY)],
        out_specs=pl.BlockSpec((1, tm, d), lambda bi, i: (bi, i, 0)),
        out_shape=jax.ShapeDtypeStruct((b, l, d), F32),
        scratch_shapes=[pltpu.VMEM((2, tm, d), F32), pltpu.SemaphoreType.DMA],
        compiler_params=_cp(("arbitrary", "arbitrary"), VMEM_LIMIT),
        name="moe_combine",
    )(dest_blk, meta, x, mgate, gf, ys)


def _moe_ffn(x, g, shift, scale, mgate, wr, br, w1, w3, w2, g_final, tm_e, tf):
    b, l, d = x.shape
    t = b * l
    ne = w1.shape[0]
    h, meta, cnt = _route(x, g, shift, scale, wr, br, min(512, l))
    e_idx = jnp.stack([meta[..., 0], meta[..., 1]]).astype(jnp.int32).reshape(2, t)
    rank = jnp.stack([meta[..., 2], meta[..., 3]]).astype(jnp.int32).reshape(2, t)
    counts = cnt[0, :ne].astype(jnp.int32)
    padded = ((counts + tm_e - 1) // tm_e) * tm_e
    ends = jnp.cumsum(padded)
    first_row = ends - padded
    dest = rank
    for e in range(ne):
        dest = dest + jnp.where(e_idx == e, first_row[e], 0)
    n_tiles = -(-2 * t // tm_e) + ne + 1
    starts = jnp.arange(n_tiles, dtype=jnp.int32) * tm_e
    tile_expert = jnp.minimum(jnp.sum(starts[:, None] >= ends[None, :], axis=1), ne - 1).astype(jnp.int32)
    n_valid = (ends[-1] // tm_e).astype(jnp.int32).reshape(1)
    pad_start = ((first_row + counts) // 8) * 8
    xs = _scatter_rows(h.reshape(t, d), dest, pad_start, n_tiles * tm_e, min(512, l), tm_e)
    ys = _experts(xs, tile_expert, n_valid, w1, w3, w2, tm_e, tf)
    return _combine(ys, dest, meta, x, mgate, g_final, min(512, l))


def _rope_tables(n_lat):
    rows = n_lat // GRID_W
    row_id = jnp.broadcast_to(jnp.arange(rows)[:, None], (rows, GRID_W)).reshape(-1).astype(F32)
    col_id = jnp.broadcast_to(jnp.arange(GRID_W)[None, :], (rows, GRID_W)).reshape(-1).astype(F32)
    q4 = RET_DH // 4
    inv = 1.0 / (ROPE_BASE ** (jnp.arange(q4, dtype=F32) / q4))
    ang_r = row_id[:, None] * inv[None, :]
    ang_c = col_id[:, None] * inv[None, :]
    zeros = jnp.zeros_like(ang_r)
    cos_h = jnp.concatenate([jnp.cos(ang_r)] * 2 + [jnp.cos(ang_c)] * 2, axis=-1)
    sin_a = jnp.concatenate([-jnp.sin(ang_r), zeros, -jnp.sin(ang_c), zeros], axis=-1)
    sin_b = jnp.concatenate([zeros, jnp.sin(ang_r), zeros, jnp.sin(ang_c)], axis=-1)
    tile = lambda t: jnp.tile(t, (1, RET_HEADS))
    return tile(cos_h), tile(sin_a), tile(sin_b)


def _zero_states(b):
    return (jnp.zeros((b, 3, LANES, LANES), F32), jnp.zeros((b, 2, LANES, LANES), F32),
            jnp.zeros((b, 2, 1, LANES), F32), jnp.zeros((b, 8, LANES), F32))


def kernel(x, c, ctx, c_ctx, w_mod, b_mod, g_mix, g_ffn, w_in, b_in, hy_conv, hy_f1, hy_fb1, hy_f2,
           hy_fb2, hy_f3, hy_decay, hy_skip, ret_decay, ml_conv, ml_gate_bias, w_up, w_out, ffn_w1,
           ffn_w3, ffn_w2, moe_router, moe_router_b, moe_w1, moe_w3, moe_w2, g_final):
    b, l, d = x.shape
    lc = ctx.shape[1]
    depth = w_mod.shape[0]
    hy_off, ret_off = 0, 3 * HY_W
    ml_off = ret_off + 4 * RET_W
    mlg_off = ml_off + 4 * ML_W
    gate_off = mlg_off + 4 * ML_HEADS

    mp = ((b + 1 + 7) // 8) * 8
    c_all = jnp.zeros((mp, d), F32).at[:b].set(c).at[b].set(c_ctx)
    mods = _mod(c_all, w_mod, b_mod).reshape(depth, mp, 6, d)

    rope = _rope_tables(l)
    tf_l, tf_c = min(512, l), min(256, lc)
    fwd_l = _dft_matrix(l, tf_l)
    fwd_c = _dft_matrix(lc, tf_c)
    tm_l = min(1024, l)
    tm_c = min(1024, b * lc)

    xl, xc = x, ctx
    for layer in range(depth):
        last = layer == depth - 1
        ml_ = mods[layer, :b][:, :, None, :]
        mc_ = mods[layer, b][None, :, None, :]
        wi = w_in[layer]
        bi = b_in[layer]
        w_main = jnp.concatenate(
            [wi[:, gate_off:], wi[:, ret_off:ml_off], wi[:, ml_off:mlg_off]], axis=1).astype(BF16)
        b_main = jnp.concatenate([bi[gate_off:], bi[ret_off:ml_off], bi[ml_off:mlg_off]])[None, :]
        w_hyt = wi[:, hy_off:ret_off].T.astype(BF16)
        b_hy = bi[hy_off:ret_off][:, None]
        w_g = jnp.zeros((d, LANES), F32).at[:, :4 * ML_HEADS].set(wi[:, mlg_off:gate_off]).astype(BF16)
        b_g = jnp.zeros((1, LANES), F32).at[0, :4 * ML_HEADS].set(bi[mlg_off:gate_off])
        gmix = g_mix[layer][None, :]

        um_l, uh_l, ug_l = _in_proj(xl, gmix, ml_[:, 0], ml_[:, 1], w_main, b_main, w_hyt, b_hy,
                                    w_g, b_g, tm_l)
        xc_flat = xc.reshape(1, b * lc, d)
        um_c, uh_c, ug_c = _in_proj(xc_flat, gmix, mc_[:, 0], mc_[:, 1], w_main, b_main, w_hyt, b_hy,
                                    w_g, b_g, tm_c)
        um_c = um_c.reshape(b, lc, N_MAIN)
        ug_c = ug_c.reshape(b, lc, LANES)

        lg = jax.nn.log_sigmoid(ret_decay[layer].astype(F32))
        gbias = jnp.zeros((1, LANES), F32).at[0, :4 * ML_HEADS].set(ml_gate_bias[layer].reshape(-1))
        qk_c = _mlqk(um_c, ml_conv[layer])
        qk_l = _mlqk(um_l, ml_conv[layer])
        z0 = _zero_states(b)
        pf_c, sf = _scan(um_c, ug_c, qk_c, None, lg[0], gbias, z0, None, False)
        yc_rm, sb = _scan(um_c, ug_c, qk_c, None, lg[1], gbias, z0, pf_c, True)
        pf_l, _ = _scan(um_l, ug_l, qk_l, rope, lg[0], gbias, sf, None, False)
        yl_rm, _ = _scan(um_l, ug_l, qk_l, rope, lg[1], gbias, sb, pf_l, True)

        w_conv_t = hy_conv[layer].T
        skip = hy_skip[layer][:, :, None]

        def hyena(uh, length, fwd, bt):
            filt = _hyena_filters(length, hy_f1[layer], hy_fb1[layer], hy_f2[layer], hy_fb2[layer],
                                  hy_f3[layer], hy_decay[layer])
            coef = _filter_coefs(filt, fwd)
            z1 = _hyena_conv(uh, 0, uh, 1, w_conv_t, skip[0], fwd, coef[0], True, bt)
            return _hyena_conv(z1, 0, uh, 2, w_conv_t, skip[1], fwd, coef[1], False, bt)

        yh_l = hyena(uh_l, l, fwd_l, 1)
        wup = w_up[layer].astype(BF16)
        wout = w_out[layer].astype(BF16)
        xl = _merge(yh_l, yl_rm[0], yl_rm[1], um_l, xl, ml_[:, 2], wup, wout, min(512, l))
        if not last:
            uh_cb = uh_c.reshape(3 * HY_W, b, lc).transpose(1, 0, 2)
            yh_c = hyena(uh_cb, lc, fwd_c, b)
            xc = _merge(yh_c, yc_rm[0], yc_rm[1], um_c, xc, mc_[:, 2], wup, wout, min(512, lc))

        j = layer // 2
        gffn = g_ffn[layer][None, :]
        gfin = g_final[None, :] if last else None
        xcf = xc.reshape(1, b * lc, d)
        if layer % 2 == 0:
            w1 = ffn_w1[j].astype(BF16)
            w3 = ffn_w3[j].astype(BF16)
            w2 = ffn_w2[j].astype(BF16)
            tf = _ff_tile(w1.shape[1])
            xl = _ffn(xl, gffn, ml_[:, 3], ml_[:, 4], ml_[:, 5], w1, w3, w2, gfin, min(1024, l), tf)
            if not last:
                xcf = _ffn(xcf, gffn, mc_[:, 3], mc_[:, 4], mc_[:, 5], w1, w3, w2, None, min(512, b * lc), tf)
        else:
            w1 = moe_w1[j].astype(BF16)
            w3 = moe_w3[j].astype(BF16)
            w2 = moe_w2[j].astype(BF16)
            wr = jnp.zeros((d, LANES), F32).at[:, :N_EXPERTS].set(moe_router[j])
            br = jnp.zeros((1, LANES), F32).at[0, :N_EXPERTS].set(moe_router_b[j])
            tf = _ff_tile(w1.shape[2])
            xl = _moe_ffn(xl, gffn, ml_[:, 3], ml_[:, 4], ml_[:, 5], wr, br, w1, w3, w2, gfin, MOE_ROWS, tf)
            if not last:
                xcf = _moe_ffn(xcf, gffn, mc_[:, 3], mc_[:, 4], mc_[:, 5], wr, br, w1, w3, w2, None, MOE_ROWS, tf)
        xc = xcf.reshape(b, lc, d)
    if depth == 0:
        raise ValueError("depth must be positive")
    return xl
```

```python
import functools
import math

import jax
import jax.numpy as jnp
from jax import lax
from jax.experimental import pallas as pl
from jax.experimental.pallas import tpu as pltpu

F32 = jnp.float32
BF16 = jnp.bfloat16
EPS = 1e-6

GRID_W = 64
HY_W = 384
RET_HEADS, RET_DH = 6, 64
RET_W = RET_HEADS * RET_DH
ML_HEADS, ML_DH = 4, 64
ML_W = ML_HEADS * ML_DH
HY_EMB = 33
ROPE_BASE = 10000.0
N_EXPERTS = 8
LANES = 128

GATE_OFF = 0
RETM_OFF = 3 * 1024
MLM_OFF = RETM_OFF + 4 * RET_W
N_MAIN = MLM_OFF + 4 * ML_W
N_TILE = 2816
CHUNK = 256

MOE_ROWS = 1024

VMEM_LIMIT = 56 * 1024 * 1024


def _ff_tile(ff):
    return ff // 2 if (ff // 2) % LANES == 0 else ff


def _cp(sem, vmem=None):
    return pltpu.CompilerParams(dimension_semantics=sem, vmem_limit_bytes=vmem)


def _split3(a):
    hi = a.astype(BF16)
    r1 = a - hi.astype(F32)
    mid = r1.astype(BF16)
    lo = (r1 - mid.astype(F32)).astype(BF16)
    return hi, mid, lo


def _dot(a, b):
    return jnp.dot(a, b, preferred_element_type=F32)


def _dot_nt(a, b):
    return lax.dot_general(a, b, (((1,), (1,)), ((), ())), preferred_element_type=F32)


def _dot3(a, b):
    ah, am, al = _split3(a)
    bh, bm, bl = _split3(b)
    return (_dot(ah, bh) + (_dot(ah, bm) + _dot(am, bh))
            + (_dot(am, bm) + _dot(ah, bl) + _dot(al, bh)))


def _sigmoid(x):
    return 1.0 / (1.0 + jnp.exp(-x))


def _silu(x):
    return x * _sigmoid(x)


def _log_sigmoid(x):
    return jnp.minimum(x, 0.0) - jnp.log(1.0 + jnp.exp(-jnp.abs(x)))


def _mod_kernel(c_ref, w_ref, b_ref, o_ref):
    s = _silu(c_ref[...])
    o_ref[0] = _dot3(s, w_ref[0]) + b_ref[0]


def _mod(c_all, w_mod, b_mod):
    depth, d, n = w_mod.shape
    mp = c_all.shape[0]
    tn = 1536
    return pl.pallas_call(
        _mod_kernel,
        grid=(depth, n // tn),
        in_specs=[pl.BlockSpec((mp, d), lambda l, j: (0, 0)),
                  pl.BlockSpec((1, d, tn), lambda l, j: (l, 0, j)),
                  pl.BlockSpec((1, 1, tn), lambda l, j: (l, 0, j))],
        out_specs=pl.BlockSpec((1, mp, tn), lambda l, j: (l, 0, j)),
        out_shape=jax.ShapeDtypeStruct((depth, mp, n), F32),
        compiler_params=_cp(("parallel", "parallel"), VMEM_LIMIT),
        name="mod",
    )(c_all, w_mod, b_mod.reshape(depth, 1, n))


def _norm_mod(x, g, shift, scale):
    y = x * lax.rsqrt(jnp.mean(x * x, axis=-1, keepdims=True) + EPS)
    return (y * g) * (1.0 + scale) + shift


def _inproj_kernel(x_ref, g_ref, sh_ref, sc_ref, wm_ref, bm_ref, wh_ref, bh_ref, wg_ref, bg_ref,
                   um_ref, uh_ref, ug_ref, h_scr):
    n = pl.program_id(2)

    @pl.when(n == 0)
    def _():
        h = _norm_mod(x_ref[0], g_ref[...], sh_ref[0], sc_ref[0]).astype(BF16)
        h_scr[...] = h
        uh_ref[0] = _dot_nt(wh_ref[...], h) + bh_ref[...]
        ug_ref[0] = _dot(h, wg_ref[...]) + bg_ref[...]

    um_ref[0] = (_dot(h_scr[...], wm_ref[...]) + bm_ref[...]).astype(BF16)


def _in_proj(x, g, shift, scale, w_main, b_main, w_hyt, b_hy, w_g, b_g, tm):
    bx, lx, d = x.shape
    per_batch_mod = shift.shape[0] > 1
    mi = (lambda b, i, n: (b, 0, 0)) if per_batch_mod else (lambda b, i, n: (0, 0, 0))
    nh = w_hyt.shape[0]
    return pl.pallas_call(
        _inproj_kernel,
        grid=(bx, lx // tm, N_MAIN // N_TILE),
        in_specs=[pl.BlockSpec((1, tm, d), lambda b, i, n: (b, i, 0)),
                  pl.BlockSpec((1, d), lambda b, i, n: (0, 0)),
                  pl.BlockSpec((1, 1, d), mi),
                  pl.BlockSpec((1, 1, d), mi),
                  pl.BlockSpec((d, N_TILE), lambda b, i, n: (0, n)),
                  pl.BlockSpec((1, N_TILE), lambda b, i, n: (0, n)),
                  pl.BlockSpec((nh, d), lambda b, i, n: (0, 0)),
                  pl.BlockSpec((nh, 1), lambda b, i, n: (0, 0)),
                  pl.BlockSpec((d, LANES), lambda b, i, n: (0, 0)),
                  pl.BlockSpec((1, LANES), lambda b, i, n: (0, 0))],
        out_specs=[pl.BlockSpec((1, tm, N_TILE), lambda b, i, n: (b, i, n)),
                   pl.BlockSpec((1, nh, tm), lambda b, i, n: (b, 0, i)),
                   pl.BlockSpec((1, tm, LANES), lambda b, i, n: (b, i, 0))],
        out_shape=[jax.ShapeDtypeStruct((bx, lx, N_MAIN), BF16),
                   jax.ShapeDtypeStruct((bx, nh, lx), F32),
                   jax.ShapeDtypeStruct((bx, lx, LANES), F32)],
        scratch_shapes=[pltpu.VMEM((tm, d), BF16)],
        compiler_params=_cp(("parallel", "parallel", "arbitrary"), VMEM_LIMIT),
        name="in_proj",
    )(x, g, shift, scale, w_main, b_main, w_hyt, b_hy, w_g, b_g)


def _dft_kernel(h_ref, f_ref, o_ref):
    hh, hm, hl = _split3(h_ref[...])
    f = f_ref[0]
    o_ref[0] = _dot(hh, f) + _dot(hm, f) + _dot(hl, f)


def _dft(h_rows, fwd):
    r, l = h_rows.shape
    nf, _, tw = fwd.shape
    tr = HY_W if r % HY_W == 0 else r
    return pl.pallas_call(
        _dft_kernel,
        grid=(nf, r // tr),
        in_specs=[pl.BlockSpec((tr, l), lambda f, i: (i, 0)),
                  pl.BlockSpec((1, l, tw), lambda f, i: (f, 0, 0))],
        out_specs=pl.BlockSpec((1, tr, tw), lambda f, i: (f, i, 0)),
        out_shape=jax.ShapeDtypeStruct((nf, r, tw), F32),
        compiler_params=_cp(("parallel", "parallel"), VMEM_LIMIT),
        name="dft",
    )(h_rows, fwd)


def _short_conv_lanes(u, w):
    l = u.shape[1]
    t = lax.broadcasted_iota(jnp.int32, u.shape, 1)
    prev = jnp.where(t == 0, 0.0, pltpu.roll(u, 1, 1))
    nxt = jnp.where(t == l - 1, 0.0, pltpu.roll(u, l - 1, 1))
    return prev * w[:, 0:1] + u * w[:, 1:2] + nxt * w[:, 2:3]


def _hyena_kernel(in_ref, gate_ref, wi_ref, wg_ref, skip_ref, f_ref, coef_ref, o_ref,
                  inb_scr, y_scr, *rest, conv_in, bt, cw, tf):
    f = pl.program_id(1)
    nf = pl.num_programs(1)

    @pl.when(f == 0)
    def _():
        for i in range(bt):
            u = in_ref[i]
            if conv_in:
                u = _short_conv_lanes(u, wi_ref[...])
                rest[0][i * cw:(i + 1) * cw, :] = u
            inb_scr[i * cw:(i + 1) * cw, :] = u.astype(BF16)
        o_ref[...] = jnp.zeros_like(o_ref)

    fw = f_ref[0]
    spec = _dot(inb_scr[...], fw)
    ca, cb, cc, cd = coef_ref[0, 0], coef_ref[0, 1], coef_ref[0, 2], coef_ref[0, 3]
    for i in range(bt):
        ur = spec[i * cw:(i + 1) * cw, :tf]
        ui = spec[i * cw:(i + 1) * cw, tf:]
        y_scr[i * cw:(i + 1) * cw, :tf] = (ur * ca + ui * cb).astype(BF16)
        y_scr[i * cw:(i + 1) * cw, tf:] = (ur * cc + ui * cd).astype(BF16)
    back = _dot_nt(y_scr[...], fw)
    for i in range(bt):
        o_ref[i] += back[i * cw:(i + 1) * cw, :]

    @pl.when(f == nf - 1)
    def _():
        for i in range(bt):
            gate = _short_conv_lanes(gate_ref[i], wg_ref[...])
            u = rest[0][i * cw:(i + 1) * cw, :] if conv_in else in_ref[i]
            o_ref[i] = gate * (o_ref[i] + skip_ref[...] * u)


def _hyena_conv(src, src_blk, gate_src, gate_blk, w_conv_t, skip, fwd, coef, conv_in, bt):
    b, _, l = src.shape
    cw = HY_W
    nf, _, tw = fwd.shape
    tf = tw // 2
    kern = functools.partial(_hyena_kernel, conv_in=conv_in, bt=bt, cw=cw, tf=tf)
    scratch = [pltpu.VMEM((bt * cw, l), BF16), pltpu.VMEM((bt * cw, tw), BF16)]
    if conv_in:
        scratch += [pltpu.VMEM((bt * cw, l), F32)]
    return pl.pallas_call(
        kern,
        grid=(b // bt, nf),
        in_specs=[pl.BlockSpec((bt, cw, l), lambda i, f: (i, src_blk, 0)),
                  pl.BlockSpec((bt, cw, l), lambda i, f: (i, gate_blk, 0)),
                  pl.BlockSpec((cw, 3), lambda i, f: (src_blk if conv_in else 0, 0)),
                  pl.BlockSpec((cw, 3), lambda i, f: (gate_blk, 0)),
                  pl.BlockSpec((cw, 1), lambda i, f: (0, 0)),
                  pl.BlockSpec((1, l, tw), lambda i, f: (f, 0, 0)),
                  pl.BlockSpec((1, 4, cw, tf), lambda i, f: (f, 0, 0, 0))],
        out_specs=pl.BlockSpec((bt, cw, l), lambda i, f: (i, 0, 0)),
        out_shape=jax.ShapeDtypeStruct((b, cw, l), F32),
        scratch_shapes=scratch,
        compiler_params=_cp(("parallel", "arbitrary"), VMEM_LIMIT),
        name="hyena_conv",
    )(src, gate_src, w_conv_t, w_conv_t, skip, fwd, coef)


def _dft_matrix(l, tf):
    n = jnp.arange(l, dtype=jnp.int32)[:, None]
    k = jnp.arange(l, dtype=jnp.int32)[None, :]
    blk = 64 if l % 64 == 0 else 1
    hi = (jnp.arange(l // blk, dtype=jnp.int32)[:, None] * blk * k) % (2 * l)
    lo = (jnp.arange(blk, dtype=jnp.int32)[:, None] * k) % (2 * l)
    ang_hi = hi.astype(F32)[:, None, :] * (math.pi / l)
    ang_lo = lo.astype(F32)[None, :, :] * (math.pi / l)
    ch, sh, cl, sl = jnp.cos(ang_hi), jnp.sin(ang_hi), jnp.cos(ang_lo), jnp.sin(ang_lo)
    re = (ch * cl - sh * sl).reshape(l, l)
    im = -(sh * cl + ch * sl).reshape(l, l)
    nyq = jnp.where(n % 2 == 0, 1.0, -1.0).astype(F32)
    im = jnp.where(k == 0, nyq, im)
    nf = l // tf
    re = re.reshape(l, nf, tf)
    im = im.reshape(l, nf, tf)
    return jnp.concatenate([re, im], axis=-1).transpose(1, 0, 2).astype(BF16)


def _hyena_filters(length, f1, fb1, f2, fb2, f3, decay):
    hp = lax.Precision.HIGHEST
    pos = jnp.arange(length, dtype=F32)
    t = pos / max(length - 1, 1)
    n_bands = (HY_EMB - 1) // 2
    bands = jnp.linspace(1e-4, n_bands - 1, n_bands, dtype=F32)
    z = (2.0 * math.pi * pos / length)[:, None] * bands[None, :]
    feats = jnp.concatenate([t[:, None], jnp.cos(z), -jnp.sin(z)], axis=-1)
    hid = jnp.sin(jnp.dot(feats, f1, precision=hp) + fb1)
    hid = jnp.sin(jnp.dot(hid, f2, precision=hp) + fb2)
    h = jnp.dot(hid, f3, precision=hp)
    window = jnp.exp(-t[:, None] * jnp.abs(decay)[None, :])
    return h * window


def _filter_coefs(filt, fwd):
    l = filt.shape[0]
    nf, _, tw = fwd.shape
    tf = tw // 2
    rows = filt.T
    is_bwd = (jnp.arange(rows.shape[0]) // HY_W) % 2 == 1
    first = jnp.arange(l)[None, :] == 0
    rows = jnp.where(is_bwd[:, None] & first, 0.0, rows)
    spec = _dft(rows, fwd)
    spec = spec.reshape(nf, 2, 2, HY_W, 2, tf)
    sf, sb = spec[:, :, 0], spec[:, :, 1]
    kre = sf[..., 0, :] + sb[..., 0, :]
    kim = sf[..., 1, :] - sb[..., 1, :]
    knyq = sf[..., 1, :] + sb[..., 1, :]
    kbin = (jnp.arange(nf)[:, None] * tf + jnp.arange(tf)[None, :])
    is0 = (kbin == 0)[:, None, None, :]
    wgt = jnp.where(is0, 1.0, 2.0) / (2.0 * l)
    ca = kre * wgt
    cb = jnp.where(is0, 0.0, -kim) * wgt
    cc = jnp.where(is0, 0.0, kim) * wgt
    cd = jnp.where(is0, knyq, kre) * wgt
    coef = jnp.stack([ca, cb, cc, cd], axis=2)
    return coef.transpose(1, 0, 2, 3, 4)


def _mlqk_kernel(u_ref, w_ref, o_ref):
    u = u_ref[0].astype(F32)
    l = u.shape[0]
    t = lax.broadcasted_iota(jnp.int32, u.shape, 0)
    prev = jnp.where(t == 0, 0.0, pltpu.roll(u, 1, 0))
    nxt = jnp.where(t == l - 1, 0.0, pltpu.roll(u, l - 1, 0))
    y = _silu(prev * w_ref[0:1, :] + u * w_ref[1:2, :] + nxt * w_ref[2:3, :])
    lane = lax.broadcasted_iota(jnp.int32, u.shape, 1)
    o_ref[0] = jnp.where(lane >= ML_W, y * (ML_DH ** -0.5), y).astype(BF16)


def _mlqk(u_main, ml_conv):
    b, l, _ = u_main.shape
    w = 2 * ML_W
    return pl.pallas_call(
        _mlqk_kernel,
        grid=(b,),
        in_specs=[pl.BlockSpec((1, l, w), lambda i: (i, 0, MLM_OFF // w)),
                  pl.BlockSpec((3, w), lambda i: (0, 0))],
        out_specs=pl.BlockSpec((1, l, w), lambda i: (i, 0, 0)),
        out_shape=jax.ShapeDtypeStruct((b, l, w), BF16),
        compiler_params=_cp(("parallel",), VMEM_LIMIT),
        name="mlqk",
    )(u_main, ml_conv)


def _head_norm_pair(xb, lo):
    def halves(v):
        s_all = jnp.sum(v, axis=1, keepdims=True)
        s_lo = jnp.sum(jnp.where(lo, v, 0.0), axis=1, keepdims=True)
        return jnp.where(lo, s_lo, s_all - s_lo) * (1.0 / 64.0)
    xc = xb - halves(xb)
    return xc * lax.rsqrt(halves(xc * xc) + EPS)


def _scan_kernel(*refs, reverse, use_rope, c):
    it = iter(refs)
    lg_ref = next(it)
    ret_ref = next(it)
    if use_rope:
        cos_ref, sa_ref, sb_ref = next(it), next(it), next(it)
    mlqk_ref = next(it)
    mlvo_ref = next(it)
    mlg_ref = next(it)
    gb_ref = next(it)
    rs0_ref, mc0_ref, mn0_ref, mm0_ref = next(it), next(it), next(it), next(it)
    if reverse:
        of_ref, hf_ref = next(it), next(it)
    o_ref, h_ref = next(it), next(it)
    rs_out, mc_out, mn_out, mm_out = next(it), next(it), next(it), next(it)
    dec_scr, qd_scr, kd_scr, tri_scr, rs_scr, mc_scr, mn_scr, mm_scr = (
        next(it), next(it), next(it), next(it), next(it), next(it), next(it), next(it))

    ci = pl.program_id(1)
    nc = pl.num_programs(1)
    d = 1 if reverse else 0

    tt = lax.broadcasted_iota(jnp.int32, (c, c), 0)
    ss = lax.broadcasted_iota(jnp.int32, (c, c), 1)
    rel = (ss - tt) if reverse else (tt - ss)
    valid = rel >= 0
    lane = lax.broadcasted_iota(jnp.int32, (c, LANES), 1)
    lo = lane < 64
    trow = lax.broadcasted_iota(jnp.int32, (c, LANES), 0)
    r128 = lax.broadcasted_iota(jnp.int32, (LANES, LANES), 0)
    c128 = lax.broadcasted_iota(jnp.int32, (LANES, LANES), 1)
    bdiag = (r128 < 64) == (c128 < 64)
    lo_row = lax.broadcasted_iota(jnp.int32, (1, LANES), 1) < 64

    @pl.when(ci == 0)
    def _():
        relf = jnp.where(valid, rel, 0).astype(F32)
        qpow = ((c - trow) if reverse else (trow + 1)).astype(F32)
        kpow = (trow if reverse else (c - 1 - trow)).astype(F32)
        for h in range(RET_HEADS):
            dec_scr[h] = jnp.where(valid, jnp.exp(relf * lg_ref[h]), 0.0)
        for p in range(RET_HEADS // 2):
            lg_l = jnp.where(lo, lg_ref[2 * p], lg_ref[2 * p + 1])
            qd_scr[p] = jnp.exp(qpow * lg_l)
            kd_scr[p] = jnp.exp(kpow * lg_l)
        tri_scr[...] = jnp.where(valid, 1.0, 0.0).astype(BF16)
        rs_scr[...] = rs0_ref[0]
        mc_scr[...] = mc0_ref[0]
        mn_scr[...] = mn0_ref[0]
        mm_scr[...] = mm0_ref[0]

    u = ret_ref[0]
    for p in range(RET_HEADS // 2):
        sl = slice(p * LANES, (p + 1) * LANES)
        q = u[:, p * LANES:(p + 1) * LANES].astype(F32) * (RET_DH ** -0.5)
        k = u[:, RET_W + p * LANES:RET_W + (p + 1) * LANES].astype(F32)
        v = u[:, 2 * RET_W + p * LANES:2 * RET_W + (p + 1) * LANES]
        if use_rope:
            cs, sa, sb = cos_ref[:, sl], sa_ref[:, sl], sb_ref[:, sl]
            q = q * cs + pltpu.roll(q, LANES - 16, 1) * sa + pltpu.roll(q, 16, 1) * sb
            k = k * cs + pltpu.roll(k, LANES - 16, 1) * sa + pltpu.roll(k, 16, 1) * sb
        qb = q.astype(BF16)
        kb = k.astype(BF16)
        outs = []
        for hh in range(2):
            qm = jnp.where(lo == (hh == 0), qb, jnp.zeros_like(qb))
            s = _dot_nt(qm, kb) * dec_scr[2 * p + hh]
            outs.append(_dot(s.astype(BF16), v))
        o_intra = jnp.where(lo, outs[0], outs[1])
        st = rs_scr[p]
        o_inter = _dot(qb, st.astype(BF16)) * qd_scr[p]
        o_new = o_intra + o_inter
        kw_t = (k * kd_scr[p]).T.astype(BF16)
        upd = _dot(kw_t, v)
        cdec = jnp.exp(jnp.where(lo_row, lg_ref[2 * p], lg_ref[2 * p + 1]) * float(c))
        rs_scr[p] = st * cdec + jnp.where(bdiag, upd, 0.0)
        if reverse:
            o_tot = of_ref[0, :, sl] + o_new
            g = u[:, 3 * RET_W + p * LANES:3 * RET_W + (p + 1) * LANES].astype(F32)
            o_ref[0, :, sl] = _head_norm_pair(o_tot, lo) * _silu(g)
        else:
            o_ref[0, :, sl] = o_new

    gates = mlg_ref[0] + gb_ref[...]
    lf = _log_sigmoid(gates)
    tri = tri_scr[...]
    l_hi, l_mid, l_lo = _split3(lf)
    cum = _dot(tri, l_hi) + _dot(tri, l_mid) + _dot(tri, l_lo)
    cum_t = cum.T
    gates_t = gates.T
    last = 0 if reverse else c - 1
    qk = mlqk_ref[0]
    vo = mlvo_ref[0]
    for p in range(ML_HEADS // 2):
        sl = slice(p * LANES, (p + 1) * LANES)
        qb = qk[:, p * LANES:(p + 1) * LANES]
        kb = qk[:, ML_W + p * LANES:ML_W + (p + 1) * LANES]
        v = vo[:, p * LANES:(p + 1) * LANES]
        cst = mc_scr[p]
        nrow = mn_scr[p]
        inter_num = _dot(qb, cst.astype(BF16))
        qf = qb.astype(F32)
        kf = kb.astype(F32)
        qn = qf * nrow
        nums, dens, a_l, m_l, wk_l, as_l = [], [], [], [], [], []
        for hh in range(2):
            h = 2 * p + hh
            ic = d * 8 + h
            fc = d * 8 + 4 + h
            b_col = cum[:, fc:fc + 1]
            b_row = cum_t[fc:fc + 1, :]
            i_row = gates_t[ic:ic + 1, :]
            i_col = gates[:, ic:ic + 1]
            m_prev = mm_scr[h:h + 1, 0:1]
            dlog = jnp.where(valid, b_col - b_row + i_row, -jnp.inf)
            inter = b_col + m_prev
            m_t = jnp.maximum(inter, jnp.max(dlog, axis=1, keepdims=True))
            w = jnp.exp(dlog - m_t)
            a = jnp.exp(inter - m_t)
            qm = jnp.where(lo == (hh == 0), qb, jnp.zeros_like(qb))
            s = _dot_nt(qm, kb) * w
            v_aug = jnp.where(lo == (hh == 0), v, jnp.ones_like(v))
            nums.append(_dot(s.astype(BF16), v_aug))
            qn_h = jnp.sum(jnp.where(lo == (hh == 0), qn, 0.0), axis=1, keepdims=True)
            dens.append(a * qn_h)
            a_l.append(a)
            m_l.append(m_t)
            total = cum[last:last + 1, fc:fc + 1]
            wlog = total - b_col + i_col
            m_new = jnp.maximum(total + m_prev, jnp.max(wlog, axis=0, keepdims=True))
            wk_l.append(jnp.exp(wlog - m_new))
            as_l.append(jnp.exp(total + m_prev - m_new))
            mm_scr[h:h + 1, :] = jnp.broadcast_to(m_new, (1, LANES))
        a_lanes = jnp.where(lo, a_l[0], a_l[1])
        num = jnp.where(lo, nums[0], nums[1]) + a_lanes * inter_num
        row_sums = pltpu.roll(jnp.where(lo, nums[1], nums[0]), 64, 1)
        den = row_sums + jnp.where(lo, dens[0], dens[1])
        m_lanes = jnp.where(lo, m_l[0], m_l[1])
        h_new = num / jnp.maximum(jnp.abs(den), jnp.exp(-m_lanes))
        kw = kf * jnp.where(lo, wk_l[0], wk_l[1])
        as_row = jnp.where(lo_row, as_l[0], as_l[1])
        upd = _dot(kw.T.astype(BF16), v)
        mc_scr[p] = cst * as_row + jnp.where(bdiag, upd, 0.0)
        mn_scr[p] = nrow * as_row + jnp.sum(kw, axis=0, keepdims=True)
        if reverse:
            h_tot = hf_ref[0, :, sl] + h_new
            og = vo[:, ML_W + p * LANES:ML_W + (p + 1) * LANES].astype(F32)
            h_ref[0, :, sl] = _sigmoid(og) * _head_norm_pair(h_tot, lo)
        else:
            h_ref[0, :, sl] = h_new

    @pl.when(ci == nc - 1)
    def _():
        rs_out[0] = rs_scr[...]
        mc_out[0] = mc_scr[...]
        mn_out[0] = mn_scr[...]
        mm_out[0] = mm_scr[...]


def _scan(u_main, u_gates, mlqk, rope, lg, gate_bias, states, prev, reverse):
    b, l, _ = u_main.shape
    c = min(CHUNK, l)
    nc = l // c
    use_rope = rope is not None
    cix = (lambda i: nc - 1 - i) if reverse else (lambda i: i)
    in_specs = [pl.BlockSpec(memory_space=pltpu.SMEM),
                pl.BlockSpec((1, c, 4 * RET_W), lambda bi, i: (bi, cix(i), RETM_OFF // (4 * RET_W)))]
    args = [lg, u_main]
    if use_rope:
        in_specs += [pl.BlockSpec((c, RET_W), lambda bi, i: (cix(i), 0))] * 3
        args += list(rope)
    in_specs += [pl.BlockSpec((1, c, 2 * ML_W), lambda bi, i: (bi, cix(i), 0)),
                 pl.BlockSpec((1, c, 2 * ML_W), lambda bi, i: (bi, cix(i), MLM_OFF // (2 * ML_W) + 1)),
                 pl.BlockSpec((1, c, LANES), lambda bi, i: (bi, cix(i), 0)),
                 pl.BlockSpec((1, LANES), lambda bi, i: (0, 0)),
                 pl.BlockSpec((1, 3, LANES, LANES), lambda bi, i: (bi, 0, 0, 0)),
                 pl.BlockSpec((1, 2, LANES, LANES), lambda bi, i: (bi, 0, 0, 0)),
                 pl.BlockSpec((1, 2, 1, LANES), lambda bi, i: (bi, 0, 0, 0)),
                 pl.BlockSpec((1, 8, LANES), lambda bi, i: (bi, 0, 0))]
    args += [mlqk, u_main, u_gates, gate_bias, *states]
    if reverse:
        in_specs += [pl.BlockSpec((1, c, RET_W), lambda bi, i: (bi, cix(i), 0)),
                     pl.BlockSpec((1, c, ML_W), lambda bi, i: (bi, cix(i), 0))]
        args += list(prev)
    out_specs = [pl.BlockSpec((1, c, RET_W), lambda bi, i: (bi, cix(i), 0)),
                 pl.BlockSpec((1, c, ML_W), lambda bi, i: (bi, cix(i), 0)),
                 pl.BlockSpec((1, 3, LANES, LANES), lambda bi, i: (bi, 0, 0, 0)),
                 pl.BlockSpec((1, 2, LANES, LANES), lambda bi, i: (bi, 0, 0, 0)),
                 pl.BlockSpec((1, 2, 1, LANES), lambda bi, i: (bi, 0, 0, 0)),
                 pl.BlockSpec((1, 8, LANES), lambda bi, i: (bi, 0, 0))]
    out_shape = [jax.ShapeDtypeStruct((b, l, RET_W), F32),
                 jax.ShapeDtypeStruct((b, l, ML_W), F32),
                 jax.ShapeDtypeStruct((b, 3, LANES, LANES), F32),
                 jax.ShapeDtypeStruct((b, 2, LANES, LANES), F32),
                 jax.ShapeDtypeStruct((b, 2, 1, LANES), F32),
                 jax.ShapeDtypeStruct((b, 8, LANES), F32)]
    scratch = [pltpu.VMEM((RET_HEADS, c, c), F32),
               pltpu.VMEM((3, c, LANES), F32),
               pltpu.VMEM((3, c, LANES), F32),
               pltpu.VMEM((c, c), BF16),
               pltpu.VMEM((3, LANES, LANES), F32),
               pltpu.VMEM((2, LANES, LANES), F32),
               pltpu.VMEM((2, 1, LANES), F32),
               pltpu.VMEM((8, LANES), F32)]
    kern = functools.partial(_scan_kernel, reverse=reverse, use_rope=use_rope, c=c)
    outs = pl.pallas_call(
        kern, grid=(b, nc), in_specs=in_specs, out_specs=out_specs, out_shape=out_shape,
        scratch_shapes=scratch,
        compiler_params=_cp(("parallel", "arbitrary"), VMEM_LIMIT),
        name="scan_bwd" if reverse else "scan_fwd",
    )(*args)
    return (outs[0], outs[1]), tuple(outs[2:])


def _merge_kernel(yh_ref, yr_ref, ym_ref, gt_ref, x_ref, mg_ref, wa_ref, wb_ref, wc_ref, wo_ref, o_ref):
    d = x_ref.shape[2]
    g = gt_ref[0].astype(F32)
    yh = yh_ref[0].T.astype(BF16)
    acc = _sigmoid(g[:, 0:d]) * _dot(yh, wa_ref[...])
    acc += _sigmoid(g[:, d:2 * d]) * _dot(yr_ref[0].astype(BF16), wb_ref[...])
    acc += _sigmoid(g[:, 2 * d:3 * d]) * _dot(ym_ref[0].astype(BF16), wc_ref[...])
    y = _dot(acc.astype(BF16), wo_ref[...])
    o_ref[0] = x_ref[0] + mg_ref[0] * y


def _merge(y_hyt, y_ret, y_ml, u_main, x, mgate, w_up, w_out, tm):
    b, l, d = x.shape
    per_batch_mod = mgate.shape[0] > 1
    mi = (lambda bi, i: (bi, 0, 0)) if per_batch_mod else (lambda bi, i: (0, 0, 0))
    wa, wb, wc = w_up[:HY_W], w_up[HY_W:HY_W + RET_W], w_up[HY_W + RET_W:]
    const = lambda bi, i: (0, 0)
    return pl.pallas_call(
        _merge_kernel,
        grid=(b, l // tm),
        in_specs=[pl.BlockSpec((1, HY_W, tm), lambda bi, i: (bi, 0, i)),
                  pl.BlockSpec((1, tm, RET_W), lambda bi, i: (bi, i, 0)),
                  pl.BlockSpec((1, tm, ML_W), lambda bi, i: (bi, i, 0)),
                  pl.BlockSpec((1, tm, 3 * d), lambda bi, i: (bi, i, 0)),
                  pl.BlockSpec((1, tm, d), lambda bi, i: (bi, i, 0)),
                  pl.BlockSpec((1, 1, d), mi),
                  pl.BlockSpec(wa.shape, const),
                  pl.BlockSpec(wb.shape, const),
                  pl.BlockSpec(wc.shape, const),
                  pl.BlockSpec(w_out.shape, const)],
        out_specs=pl.BlockSpec((1, tm, d), lambda bi, i: (bi, i, 0)),
        out_shape=jax.ShapeDtypeStruct((b, l, d), F32),
        compiler_params=_cp(("parallel", "parallel"), VMEM_LIMIT),
        name="merge",
    )(y_hyt, y_ret, y_ml, u_main, x, mgate, wa, wb, wc, w_out)


def _ffn_kernel(*refs, final):
    it = iter(refs)
    x_ref, g_ref, sh_ref, sc_ref, mg_ref = next(it), next(it), next(it), next(it), next(it)
    w1_ref, w3_ref, w2_ref = next(it), next(it), next(it)
    if final:
        gf_ref = next(it)
    o_ref = next(it)
    h_scr, acc_scr = next(it), next(it)
    f = pl.program_id(2)

    @pl.when(f == 0)
    def _():
        h = _norm_mod(x_ref[0], g_ref[...], sh_ref[0], sc_ref[0])
        h_scr[...] = h.astype(BF16)
        acc_scr[...] = jnp.zeros_like(acc_scr)

    h = h_scr[...]
    act = _silu(_dot(h, w1_ref[...])) * _dot(h, w3_ref[...])
    acc_scr[...] += _dot(act.astype(BF16), w2_ref[...])

    @pl.when(f == pl.num_programs(2) - 1)
    def _():
        y = x_ref[0] + mg_ref[0] * acc_scr[...]
        if final:
            y = y * lax.rsqrt(jnp.mean(y * y, axis=-1, keepdims=True) + EPS) * gf_ref[...]
        o_ref[0] = y


def _ffn(x, g, shift, scale, mgate, w1, w3, w2, g_final, tm, tf):
    b, l, d = x.shape
    ff = w1.shape[1]
    final = g_final is not None
    per_batch_mod = shift.shape[0] > 1
    mi = (lambda bi, i, f: (bi, 0, 0)) if per_batch_mod else (lambda bi, i, f: (0, 0, 0))
    const2 = lambda bi, i, f: (0, 0)
    in_specs = [pl.BlockSpec((1, tm, d), lambda bi, i, f: (bi, i, 0)),
                pl.BlockSpec((1, d), const2),
                pl.BlockSpec((1, 1, d), mi), pl.BlockSpec((1, 1, d), mi), pl.BlockSpec((1, 1, d), mi),
                pl.BlockSpec((d, tf), lambda bi, i, f: (0, f)),
                pl.BlockSpec((d, tf), lambda bi, i, f: (0, f)),
                pl.BlockSpec((tf, d), lambda bi, i, f: (f, 0))]
    args = [x, g, shift, scale, mgate, w1, w3, w2]
    if final:
        in_specs += [pl.BlockSpec((1, d), const2)]
        args += [g_final]
    return pl.pallas_call(
        functools.partial(_ffn_kernel, final=final),
        grid=(b, l // tm, ff // tf),
        in_specs=in_specs,
        out_specs=pl.BlockSpec((1, tm, d), lambda bi, i, f: (bi, i, 0)),
        out_shape=jax.ShapeDtypeStruct((b, l, d), F32),
        scratch_shapes=[pltpu.VMEM((tm, d), BF16), pltpu.VMEM((tm, d), F32)],
        compiler_params=_cp(("parallel", "parallel", "arbitrary"), VMEM_LIMIT),
        name="ffn",
    )(*args)


def _top2(logits):
    lane = lax.broadcasted_iota(jnp.int32, logits.shape, 1)
    neg = jnp.float32(-jnp.inf)
    logits = jnp.where(lane < N_EXPERTS, logits, neg)
    m1 = jnp.max(logits, axis=1, keepdims=True)
    i1 = jnp.min(jnp.where(logits == m1, lane, LANES), axis=1, keepdims=True)
    rest = jnp.where(lane == i1, neg, logits)
    m2 = jnp.max(rest, axis=1, keepdims=True)
    i2 = jnp.min(jnp.where(rest == m2, lane, LANES), axis=1, keepdims=True)
    e2 = jnp.exp(m2 - m1)
    return lane, i1, i2, 1.0 / (1.0 + e2), e2 / (1.0 + e2)


def _route_kernel(x_ref, g_ref, sh_ref, sc_ref, wr_ref, br_ref, h_ref, meta_ref, cnt_ref, cnt_scr):
    first = jnp.logical_and(pl.program_id(0) == 0, pl.program_id(1) == 0)

    @pl.when(first)
    def _():
        cnt_scr[...] = jnp.zeros_like(cnt_scr)

    h = _norm_mod(x_ref[0], g_ref[...], sh_ref[0], sc_ref[0])
    h_ref[0] = h
    tm = h.shape[0]
    lane, i1, i2, w1, w2 = _top2(_dot3(h, wr_ref[...]) + br_ref[...])
    oh1 = (lane == i1).astype(F32)
    oh2 = (lane == i2).astype(F32)
    both = oh1 + oh2
    tt = lax.broadcasted_iota(jnp.int32, (tm, tm), 0)
    ss = lax.broadcasted_iota(jnp.int32, (tm, tm), 1)
    before = jnp.where(tt > ss, 1.0, 0.0).astype(BF16)
    pref = _dot(before, both.astype(BF16)) + cnt_scr[...]
    r1 = jnp.sum(pref * oh1, axis=1, keepdims=True)
    r2 = jnp.sum(pref * oh2, axis=1, keepdims=True)
    cnt_scr[...] += jnp.sum(both, axis=0, keepdims=True)
    cnt_ref[...] = cnt_scr[...]
    vals = (i1.astype(F32), i2.astype(F32), r1, r2, w1, w2)
    meta = jnp.zeros((tm, LANES), F32)
    for j, val in enumerate(vals):
        meta = jnp.where(lane == j, val, meta)
    meta_ref[0] = meta[:, :8]


def _route(x, g, shift, scale, wr, br, tm):
    b, l, d = x.shape
    const2 = lambda bi, i: (0, 0)
    return pl.pallas_call(
        _route_kernel,
        grid=(b, l // tm),
        in_specs=[pl.BlockSpec((1, tm, d), lambda bi, i: (bi, i, 0)),
                  pl.BlockSpec((1, d), const2),
                  pl.BlockSpec((1, 1, d), lambda bi, i: (bi, 0, 0)),
                  pl.BlockSpec((1, 1, d), lambda bi, i: (bi, 0, 0)),
                  pl.BlockSpec((d, LANES), const2),
                  pl.BlockSpec((1, LANES), const2)],
        out_specs=[pl.BlockSpec((1, tm, d), lambda bi, i: (bi, i, 0)),
                   pl.BlockSpec((1, tm, 8), lambda bi, i: (bi, i, 0)),
                   pl.BlockSpec((1, LANES), const2)],
        out_shape=[jax.ShapeDtypeStruct((b, l, d), F32),
                   jax.ShapeDtypeStruct((b, l, 8), F32),
                   jax.ShapeDtypeStruct((1, LANES), F32)],
        scratch_shapes=[pltpu.VMEM((1, LANES), F32)],
        compiler_params=_cp(("arbitrary", "arbitrary"), VMEM_LIMIT),
        name="route",
    )(x, g, shift, scale, wr, br)


def _row_copy(src_ref, src_row, dst_ref, dst_row, sem):
    return pltpu.make_async_copy(src_ref.at[pl.ds(src_row, 1)], dst_ref.at[pl.ds(dst_row, 1)], sem)


ROW_UNROLL = 8


def _scatter_kernel(pad_ref, dest_ref, h_ref, xs_ref, zero_scr, sem):
    tm = h_ref.shape[0]
    zr = zero_scr.shape[0]

    @pl.when(pl.program_id(0) == 0)
    def _():
        zero_scr[...] = jnp.zeros_like(zero_scr)
        tail = [xs_ref.shape[0] - (j + 1) * zr for j in range(N_EXPERTS + 1)]
        starts = [pl.multiple_of(pad_ref[e], 8) for e in range(N_EXPERTS)]
        for start in tail:
            pltpu.make_async_copy(zero_scr, xs_ref.at[pl.ds(start, zr)], sem).start()
        for start in tail:
            pltpu.make_async_copy(zero_scr, xs_ref.at[pl.ds(start, zr)], sem).wait()
        for start in starts:
            pltpu.make_async_copy(zero_scr, xs_ref.at[pl.ds(start, zr)], sem).start()
        for start in starts:
            pltpu.make_async_copy(zero_scr, xs_ref.at[pl.ds(start, zr)], sem).wait()

    def issue(j, carry):
        for k in range(ROW_UNROLL):
            r = j * ROW_UNROLL + k
            _row_copy(h_ref, r, xs_ref, dest_ref[0, 0, r], sem).start(priority=0)
            _row_copy(h_ref, r, xs_ref, dest_ref[0, 0, tm + r], sem).start(priority=1)
        return carry

    lax.fori_loop(0, tm // ROW_UNROLL, issue, 0)
    for _ in range(2):
        pltpu.make_async_copy(h_ref, xs_ref.at[pl.ds(0, tm)], sem).wait()


def _scatter_rows(h_flat, dest, pad_start, n_rows, tm, zr):
    t, d = h_flat.shape
    nt = t // tm
    dest_blk = jnp.concatenate([dest[0].reshape(nt, 1, tm), dest[1].reshape(nt, 1, tm)], axis=2)
    return pl.pallas_call(
        _scatter_kernel,
        grid=(nt,),
        in_specs=[pl.BlockSpec(memory_space=pltpu.SMEM),
                  pl.BlockSpec((1, 1, 2 * tm), lambda i: (i, 0, 0), memory_space=pltpu.SMEM),
                  pl.BlockSpec((tm, d), lambda i: (i, 0))],
        out_specs=pl.BlockSpec(memory_space=pl.ANY),
        out_shape=jax.ShapeDtypeStruct((n_rows, d), F32),
        scratch_shapes=[pltpu.VMEM((zr, d), F32), pltpu.SemaphoreType.DMA],
        compiler_params=_cp(("arbitrary",), VMEM_LIMIT),
        name="moe_scatter",
    )(pad_start, dest_blk, h_flat)


def _experts_kernel(te_ref, nv_ref, x_ref, w1_ref, w3_ref, w2_ref, o_ref, h_scr, acc_scr):
    del te_ref
    i = pl.program_id(0)
    f = pl.program_id(1)
    live = i < nv_ref[0]

    @pl.when(jnp.logical_and(live, f == 0))
    def _():
        h_scr[...] = x_ref[...].astype(BF16)
        acc_scr[...] = jnp.zeros_like(acc_scr)

    @pl.when(live)
    def _():
        h = h_scr[...]
        act = _silu(_dot(h, w1_ref[0])) * _dot(h, w3_ref[0])
        acc_scr[...] += _dot(act.astype(BF16), w2_ref[0])

    @pl.when(f == pl.num_programs(1) - 1)
    def _():
        o_ref[...] = jnp.where(live, acc_scr[...], 0.0)


def _experts(xs, tile_expert, n_valid, w1, w3, w2, tm, tf):
    p, d = xs.shape
    ff = w1.shape[2]
    nt = p // tm
    row = lambda i, f, te, nv: (jnp.minimum(i, nv[0] - 1), 0)
    grid_spec = pltpu.PrefetchScalarGridSpec(
        num_scalar_prefetch=2,
        grid=(nt, ff // tf),
        in_specs=[pl.BlockSpec((tm, d), row),
                  pl.BlockSpec((1, d, tf), lambda i, f, te, nv: (te[i], 0, f)),
                  pl.BlockSpec((1, d, tf), lambda i, f, te, nv: (te[i], 0, f)),
                  pl.BlockSpec((1, tf, d), lambda i, f, te, nv: (te[i], f, 0))],
        out_specs=pl.BlockSpec((tm, d), lambda i, f, te, nv: (i, 0)),
        scratch_shapes=[pltpu.VMEM((tm, d), BF16), pltpu.VMEM((tm, d), F32)])
    return pl.pallas_call(
        _experts_kernel,
        grid_spec=grid_spec,
        out_shape=jax.ShapeDtypeStruct((p, d), F32),
        compiler_params=_cp(("arbitrary", "arbitrary"), VMEM_LIMIT),
        name="moe_experts",
    )(tile_expert, n_valid, xs, w1, w3, w2)


def _combine_kernel(dest_ref, meta_ref, x_ref, mg_ref, gf_ref, ys_ref, o_ref, ybuf, sem, *, final):
    tm = x_ref.shape[1]

    def issue(j, carry):
        for k in range(ROW_UNROLL):
            r = j * ROW_UNROLL + k
            _row_copy(ys_ref, dest_ref[0, 0, r], ybuf.at[0], r, sem).start(priority=0)
            _row_copy(ys_ref, dest_ref[0, 0, tm + r], ybuf.at[1], r, sem).start(priority=1)
        return carry

    lax.fori_loop(0, tm // ROW_UNROLL, issue, 0)
    for slot in range(2):
        pltpu.make_async_copy(ys_ref.at[pl.ds(0, tm)], ybuf.at[slot], sem).wait()
    meta = meta_ref[0]
    y = x_ref[0] + mg_ref[0] * (meta[:, 4:5] * ybuf[0] + meta[:, 5:6] * ybuf[1])
    if final:
        y = y * lax.rsqrt(jnp.mean(y * y, axis=-1, keepdims=True) + EPS) * gf_ref[...]
    o_ref[0] = y


def _combine(ys, dest, meta, x, mgate, g_final, tm):
    b, l, d = x.shape
    nt = l // tm
    final = g_final is not None
    gf = g_final if final else jnp.ones((1, d), F32)
    dest_blk = jnp.concatenate([dest[0].reshape(b * nt, 1, tm), dest[1].reshape(b * nt, 1, tm)], axis=2)
    return pl.pallas_call(
        functools.partial(_combine_kernel, final=final),
        grid=(b, nt),
        in_specs=[pl.BlockSpec((1, 1, 2 * tm), lambda bi, i: (bi * nt + i, 0, 0), memory_space=pltpu.SMEM),
                  pl.BlockSpec((1, tm, 8), lambda bi, i: (bi, i, 0)),
                  pl.BlockSpec((1, tm, d), lambda bi, i: (bi, i, 0)),
                  pl.BlockSpec((1, 1, d), lambda bi, i: (bi, 0, 0)),
                  pl.BlockSpec((1, d), lambda bi, i: (0, 0)),
                  pl.BlockSpec(memory_space=pl.ANY)],
        out_specs=pl.BlockSpec((1, tm, d), lambda bi, i: (bi, i, 0)),
        out_shape=jax.ShapeDtypeStruct((b, l, d), F32),
        scratch_shapes=[pltpu.VMEM((2, tm, d), F32), pltpu.SemaphoreType.DMA],
        compiler_params=_cp(("arbitrary", "arbitrary"), VMEM_LIMIT),
        name="moe_combine",
    )(dest_blk, meta, x, mgate, gf, ys)


def _moe_ffn(x, g, shift, scale, mgate, wr, br, w1, w3, w2, g_final, tm_e, tf):
    b, l, d = x.shape
    t = b * l
    ne = w1.shape[0]
    h, meta, cnt = _route(x, g, shift, scale, wr, br, min(512, l))
    e_idx = jnp.stack([meta[..., 0], meta[..., 1]]).astype(jnp.int32).reshape(2, t)
    rank = jnp.stack([meta[..., 2], meta[..., 3]]).astype(jnp.int32).reshape(2, t)
    counts = cnt[0, :ne].astype(jnp.int32)
    padded = ((counts + tm_e - 1) // tm_e) * tm_e
    ends = jnp.cumsum(padded)
    first_row = ends - padded
    dest = rank
    for e in range(ne):
        dest = dest + jnp.where(e_idx == e, first_row[e], 0)
    n_tiles = -(-2 * t // tm_e) + ne + 1
    starts = jnp.arange(n_tiles, dtype=jnp.int32) * tm_e
    tile_expert = jnp.minimum(jnp.sum(starts[:, None] >= ends[None, :], axis=1), ne - 1).astype(jnp.int32)
    n_valid = (ends[-1] // tm_e).astype(jnp.int32).reshape(1)
    pad_start = ((first_row + counts) // 8) * 8
    xs = _scatter_rows(h.reshape(t, d), dest, pad_start, n_tiles * tm_e, min(512, l), tm_e)
    ys = _experts(xs, tile_expert, n_valid, w1, w3, w2, tm_e, tf)
    return _combine(ys, dest, meta, x, mgate, g_final, min(512, l))


def _rope_tables(n_lat):
    rows = n_lat // GRID_W
    row_id = jnp.broadcast_to(jnp.arange(rows)[:, None], (rows, GRID_W)).reshape(-1).astype(F32)
    col_id = jnp.broadcast_to(jnp.arange(GRID_W)[None, :], (rows, GRID_W)).reshape(-1).astype(F32)
    q4 = RET_DH // 4
    inv = 1.0 / (ROPE_BASE ** (jnp.arange(q4, dtype=F32) / q4))
    ang_r = row_id[:, None] * inv[None, :]
    ang_c = col_id[:, None] * inv[None, :]
    zeros = jnp.zeros_like(ang_r)
    cos_h = jnp.concatenate([jnp.cos(ang_r)] * 2 + [jnp.cos(ang_c)] * 2, axis=-1)
    sin_a = jnp.concatenate([-jnp.sin(ang_r), zeros, -jnp.sin(ang_c), zeros], axis=-1)
    sin_b = jnp.concatenate([zeros, jnp.sin(ang_r), zeros, jnp.sin(ang_c)], axis=-1)
    tile = lambda t: jnp.tile(t, (1, RET_HEADS))
    return tile(cos_h), tile(sin_a), tile(sin_b)


def _zero_states(b):
    return (jnp.zeros((b, 3, LANES, LANES), F32), jnp.zeros((b, 2, LANES, LANES), F32),
            jnp.zeros((b, 2, 1, LANES), F32), jnp.zeros((b, 8, LANES), F32))


def kernel(x, c, ctx, c_ctx, w_mod, b_mod, g_mix, g_ffn, w_in, b_in, hy_conv, hy_f1, hy_fb1, hy_f2,
           hy_fb2, hy_f3, hy_decay, hy_skip, ret_decay, ml_conv, ml_gate_bias, w_up, w_out, ffn_w1,
           ffn_w3, ffn_w2, moe_router, moe_router_b, moe_w1, moe_w3, moe_w2, g_final):
    b, l, d = x.shape
    lc = ctx.shape[1]
    depth = w_mod.shape[0]
    hy_off, ret_off = 0, 3 * HY_W
    ml_off = ret_off + 4 * RET_W
    mlg_off = ml_off + 4 * ML_W
    gate_off = mlg_off + 4 * ML_HEADS

    mp = ((b + 1 + 7) // 8) * 8
    c_all = jnp.zeros((mp, d), F32).at[:b].set(c).at[b].set(c_ctx)
    mods = _mod(c_all, w_mod, b_mod).reshape(depth, mp, 6, d)

    rope = _rope_tables(l)
    tf_l, tf_c = min(512, l), min(256, lc)
    fwd_l = _dft_matrix(l, tf_l)
    fwd_c = _dft_matrix(lc, tf_c)
    tm_l = min(1024, l)
    tm_c = min(1024, b * lc)

    xl, xc = x, ctx
    for layer in range(depth):
        last = layer == depth - 1
        ml_ = mods[layer, :b][:, :, None, :]
        mc_ = mods[layer, b][None, :, None, :]
        wi = w_in[layer]
        bi = b_in[layer]
        w_main = jnp.concatenate(
            [wi[:, gate_off:], wi[:, ret_off:ml_off], wi[:, ml_off:mlg_off]], axis=1).astype(BF16)
        b_main = jnp.concatenate([bi[gate_off:], bi[ret_off:ml_off], bi[ml_off:mlg_off]])[None, :]
        w_hyt = wi[:, hy_off:ret_off].T.astype(BF16)
        b_hy = bi[hy_off:ret_off][:, None]
        w_g = jnp.zeros((d, LANES), F32).at[:, :4 * ML_HEADS].set(wi[:, mlg_off:gate_off]).astype(BF16)
        b_g = jnp.zeros((1, LANES), F32).at[0, :4 * ML_HEADS].set(bi[mlg_off:gate_off])
        gmix = g_mix[layer][None, :]

        um_l, uh_l, ug_l = _in_proj(xl, gmix, ml_[:, 0], ml_[:, 1], w_main, b_main, w_hyt, b_hy,
                                    w_g, b_g, tm_l)
        xc_flat = xc.reshape(1, b * lc, d)
        um_c, uh_c, ug_c = _in_proj(xc_flat, gmix, mc_[:, 0], mc_[:, 1], w_main, b_main, w_hyt, b_hy,
                                    w_g, b_g, tm_c)
        um_c = um_c.reshape(b, lc, N_MAIN)
        ug_c = ug_c.reshape(b, lc, LANES)

        lg = jax.nn.log_sigmoid(ret_decay[layer].astype(F32))
        gbias = jnp.zeros((1, LANES), F32).at[0, :4 * ML_HEADS].set(ml_gate_bias[layer].reshape(-1))
        qk_c = _mlqk(um_c, ml_conv[layer])
        qk_l = _mlqk(um_l, ml_conv[layer])
        z0 = _zero_states(b)
        pf_c, sf = _scan(um_c, ug_c, qk_c, None, lg[0], gbias, z0, None, False)
        yc_rm, sb = _scan(um_c, ug_c, qk_c, None, lg[1], gbias, z0, pf_c, True)
        pf_l, _ = _scan(um_l, ug_l, qk_l, rope, lg[0], gbias, sf, None, False)
        yl_rm, _ = _scan(um_l, ug_l, qk_l, rope, lg[1], gbias, sb, pf_l, True)

        w_conv_t = hy_conv[layer].T
        skip = hy_skip[layer][:, :, None]

        def hyena(uh, length, fwd, bt):
            filt = _hyena_filters(length, hy_f1[layer], hy_fb1[layer], hy_f2[layer], hy_fb2[layer],
                                  hy_f3[layer], hy_decay[layer])
            coef = _filter_coefs(filt, fwd)
            z1 = _hyena_conv(uh, 0, uh, 1, w_conv_t, skip[0], fwd, coef[0], True, bt)
            return _hyena_conv(z1, 0, uh, 2, w_conv_t, skip[1], fwd, coef[1], False, bt)

        yh_l = hyena(uh_l, l, fwd_l, 1)
        wup = w_up[layer].astype(BF16)
        wout = w_out[layer].astype(BF16)
        xl = _merge(yh_l, yl_rm[0], yl_rm[1], um_l, xl, ml_[:, 2], wup, wout, min(512, l))
        if not last:
            uh_cb = uh_c.reshape(3 * HY_W, b, lc).transpose(1, 0, 2)
            yh_c = hyena(uh_cb, lc, fwd_c, b)
            xc = _merge(yh_c, yc_rm[0], yc_rm[1], um_c, xc, mc_[:, 2], wup, wout, min(512, lc))

        j = layer // 2
        gffn = g_ffn[layer][None, :]
        gfin = g_final[None, :] if last else None
        xcf = xc.reshape(1, b * lc, d)
        if layer % 2 == 0:
            w1 = ffn_w1[j].astype(BF16)
            w3 = ffn_w3[j].astype(BF16)
            w2 = ffn_w2[j].astype(BF16)
            tf = _ff_tile(w1.shape[1])
            xl = _ffn(xl, gffn, ml_[:, 3], ml_[:, 4], ml_[:, 5], w1, w3, w2, gfin, min(1024, l), tf)
            if not last:
                xcf = _ffn(xcf, gffn, mc_[:, 3], mc_[:, 4], mc_[:, 5], w1, w3, w2, None, min(512, b * lc), tf)
        else:
            w1 = moe_w1[j].astype(BF16)
            w3 = moe_w3[j].astype(BF16)
            w2 = moe_w2[j].astype(BF16)
            wr = jnp.zeros((d, LANES), F32).at[:, :N_EXPERTS].set(moe_router[j])
            br = jnp.zeros((1, LANES), F32).at[0, :N_EXPERTS].set(moe_router_b[j])
            tf = _ff_tile(w1.shape[2])
            xl = _moe_ffn(xl, gffn, ml_[:, 3], ml_[:, 4], ml_[:, 5], wr, br, w1, w3, w2, gfin, MOE_ROWS, tf)
            if not last:
                xcf = _moe_ffn(xcf, gffn, mc_[:, 3], mc_[:, 4], mc_[:, 5], wr, br, w1, w3, w2, None, MOE_ROWS, tf)
        xc = xcf.reshape(b, lc, d)
    if depth == 0:
        raise ValueError("depth must be positive")
    return xl
```
